```python
import math
import jax, jax.numpy as jnp
from jax import lax
import numpy as np

D_MODEL = 2048
BATCH = 8
SEQ = 4096
DEPTH = 4

N_EVEN = (DEPTH + 1) // 2
N_ODD = DEPTH // 2
A_CH = D_MODEL // 2
A_GROUPS = 8
CONV_A_WIDTH = 31
B_CH = D_MODEL // 2
B_HEADS = 8
B_HEAD_DIM = B_CH // B_HEADS
CHUNK = 128
MIX_IN = 2 * A_CH + 2 * B_CH
DA_HEADS = 8
DA_HEAD_DIM = 128
DA_V_DIM = 2 * DA_HEAD_DIM
QK_W = DA_HEADS * 2 * DA_HEAD_DIM
V_W = DA_HEADS * DA_V_DIM
Q_BLOCK = 128
D_FF = 5632
FFN_CONV_WIDTH = 3
EPS = 1e-6
NEG_INF = -1e30

kernel_name = "hybrid_conformer_gmlp_diffattn_convffn"


def rmsnorm(x, g):
    xf = x.astype(jnp.float32)
    y = xf * lax.rsqrt(jnp.mean(xf * xf, axis=-1, keepdims=True) + EPS)
    return (y * g.astype(jnp.float32)).astype(x.dtype)


def layernorm(x, g, b):
    xf = x.astype(jnp.float32)
    mu = jnp.mean(xf, axis=-1, keepdims=True)
    xc = xf - mu
    var = jnp.mean(xc * xc, axis=-1, keepdims=True)
    y = xc * lax.rsqrt(var + EPS) * g.astype(jnp.float32) + b.astype(jnp.float32)
    return y.astype(x.dtype)


def causal_dwconv(x, w, b):
    width = w.shape[0]
    y = lax.conv_general_dilated(
        x, w[:, None, :].astype(x.dtype), window_strides=(1,),
        padding=[(width - 1, 0)], dimension_numbers=("NWC", "WIO", "NWC"),
        feature_group_count=x.shape[-1])
    return y + b.astype(x.dtype)


def conv_gmlp_mixer(h, w_in, conv_w, conv_b, ln_a_g, ln_a_b, ln_v_g, ln_v_b, w_s, b_s, w_out):
    bsz, seq, _ = h.shape
    z = h @ w_in
    a_val, a_gate, u, v = jnp.split(z, [A_CH, 2 * A_CH, 2 * A_CH + B_CH], axis=-1)
    a = a_val * jax.nn.sigmoid(a_gate)
    a = causal_dwconv(a, conv_w, conv_b)
    a = jax.nn.silu(layernorm(a, ln_a_g, ln_a_b))
    u = jax.nn.gelu(u)
    v = layernorm(jax.nn.gelu(v), ln_v_g, ln_v_b)
    v = v.reshape(bsz, seq // CHUNK, CHUNK, B_HEADS, B_HEAD_DIM)
    causal = jnp.tril(jnp.ones((CHUNK, CHUNK), dtype=bool))
    w_m = jnp.where(causal[None], w_s, jnp.zeros_like(w_s)).astype(v.dtype)
    s = jnp.einsum("hts,bcshd->bcthd", w_m, v) + b_s.T.astype(v.dtype)[None, None, :, :, None]
    bo = u * s.reshape(bsz, seq, B_CH)
    return jnp.concatenate([a, bo], axis=-1) @ w_out


def diff_attention(h, w_qkv, lq1, lk1, lq2, lk2, subln_g, w_o, lambda_init):
    bsz, seq, _ = h.shape
    nb = seq // Q_BLOCK
    qkv = h @ w_qkv
    q, k, v = jnp.split(qkv, [QK_W, 2 * QK_W], axis=-1)
    q = q.reshape(bsz, seq, DA_HEADS, 2, DA_HEAD_DIM)
    k = k.reshape(bsz, seq, DA_HEADS, 2, DA_HEAD_DIM)
    v = v.reshape(bsz, seq, DA_HEADS, DA_V_DIM)
    lam = (jnp.exp(jnp.sum(lq1.astype(jnp.float32) * lk1.astype(jnp.float32)))
           - jnp.exp(jnp.sum(lq2.astype(jnp.float32) * lk2.astype(jnp.float32)))
           + lambda_init)
    scale = 1.0 / math.sqrt(DA_HEAD_DIM)
    kpos = jnp.arange(seq)
    qblocks = jnp.moveaxis(q.reshape(bsz, nb, Q_BLOCK, DA_HEADS, 2, DA_HEAD_DIM), 1, 0)

    def attend_block(args):
        i, qb = args
        sc = jnp.einsum("bqhcd,bkhcd->bhcqk", qb, k,
                        preferred_element_type=jnp.float32) * scale
        qpos = i * Q_BLOCK + jnp.arange(Q_BLOCK)
        mask = kpos[None, :] <= qpos[:, None]
        sc = jnp.where(mask, sc, NEG_INF)
        p = jax.nn.softmax(sc, axis=-1)
        a = p[:, :, 0] - lam * p[:, :, 1]
        return jnp.einsum("bhqk,bkhe->bqhe", a.astype(v.dtype), v)

    o = lax.map(attend_block, (jnp.arange(nb), qblocks))
    o = jnp.moveaxis(o, 0, 1).reshape(bsz, seq, DA_HEADS, DA_V_DIM)
    o = rmsnorm(o, subln_g) * (1.0 - lambda_init)
    return o.reshape(bsz, seq, V_W) @ w_o


def conv_ffn(h, w_up, conv_w, conv_b, w_down):
    z = causal_dwconv(h @ w_up, conv_w, conv_b)
    gate, val = jnp.split(z, [D_FF], axis=-1)
    return (jax.nn.silu(gate) * val) @ w_down


def setup_inputs(seed: int = 0) -> dict:
    key = jax.random.key(seed)
    ks = jax.random.split(key, 32)
    f32 = jnp.float32

    def nrm(k, shape, scale):
        return jax.random.normal(k, shape, f32) * scale

    return {
        "x": nrm(ks[0], (BATCH, SEQ, D_MODEL), 1.0),
        "norm_mix_g": 1.0 + nrm(ks[1], (DEPTH, D_MODEL), 0.02),
        "norm_ffn_g": 1.0 + nrm(ks[2], (DEPTH, D_MODEL), 0.02),
        "final_norm_g": 1.0 + nrm(ks[3], (D_MODEL,), 0.02),
        "ev_w_in": nrm(ks[4], (N_EVEN, D_MODEL, MIX_IN), D_MODEL ** -0.5),
        "ev_conv_w": nrm(ks[5], (N_EVEN, CONV_A_WIDTH, A_CH), CONV_A_WIDTH ** -0.5),
        "ev_conv_b": nrm(ks[6], (N_EVEN, A_CH), 0.02),
        "ev_ln_a_g": 1.0 + nrm(ks[7], (N_EVEN, A_CH), 0.02),
        "ev_ln_a_b": nrm(ks[8], (N_EVEN, A_CH), 0.02),
        "ev_ln_v_g": 1.0 + nrm(ks[9], (N_EVEN, B_CH), 0.02),
        "ev_ln_v_b": nrm(ks[10], (N_EVEN, B_CH), 0.02),
        "ev_w_s": nrm(ks[11], (N_EVEN, B_HEADS, CHUNK, CHUNK), CHUNK ** -0.5),
        "ev_b_s": 1.0 + nrm(ks[12], (N_EVEN, B_HEADS, CHUNK), 0.02),
        "ev_w_out": nrm(ks[13], (N_EVEN, A_CH + B_CH, D_MODEL), (A_CH + B_CH) ** -0.5),
        "od_w_qkv": nrm(ks[14], (N_ODD, D_MODEL, 2 * QK_W + V_W), D_MODEL ** -0.5),
        "od_lambda_q1": nrm(ks[15], (N_ODD, DA_HEAD_DIM), 0.1),
        "od_lambda_k1": nrm(ks[16], (N_ODD, DA_HEAD_DIM), 0.1),
        "od_lambda_q2": nrm(ks[17], (N_ODD, DA_HEAD_DIM), 0.1),
        "od_lambda_k2": nrm(ks[18], (N_ODD, DA_HEAD_DIM), 0.1),
        "od_subln_g": 1.0 + nrm(ks[19], (N_ODD, DA_V_DIM), 0.02),
        "od_w_o": nrm(ks[20], (N_ODD, V_W, D_MODEL), V_W ** -0.5),
        "ffn_w_up": nrm(ks[21], (DEPTH, D_MODEL, 2 * D_FF), D_MODEL ** -0.5),
        "ffn_conv_w": nrm(ks[22], (DEPTH, FFN_CONV_WIDTH, 2 * D_FF), FFN_CONV_WIDTH ** -0.5),
        "ffn_conv_b": nrm(ks[23], (DEPTH, 2 * D_FF), 0.02),
        "ffn_w_down": nrm(ks[24], (DEPTH, D_FF, D_MODEL), D_FF ** -0.5),
    }


def reference(x, norm_mix_g, norm_ffn_g, final_norm_g,
              ev_w_in, ev_conv_w, ev_conv_b, ev_ln_a_g, ev_ln_a_b, ev_ln_v_g, ev_ln_v_b,
              ev_w_s, ev_b_s, ev_w_out,
              od_w_qkv, od_lambda_q1, od_lambda_k1, od_lambda_q2, od_lambda_k2, od_subln_g, od_w_o,
              ffn_w_up, ffn_conv_w, ffn_conv_b, ffn_w_down):
    for i in range(DEPTH):
        j = i // 2
        h = rmsnorm(x, norm_mix_g[i])
        if i % 2 == 0:
            x = x + conv_gmlp_mixer(h, ev_w_in[j], ev_conv_w[j], ev_conv_b[j],
                                    ev_ln_a_g[j], ev_ln_a_b[j], ev_ln_v_g[j], ev_ln_v_b[j],
                                    ev_w_s[j], ev_b_s[j], ev_w_out[j])
        else:
            lambda_init = 0.8 - 0.6 * math.exp(-0.3 * i)
            x = x + diff_attention(h, od_w_qkv[j], od_lambda_q1[j], od_lambda_k1[j],
                                   od_lambda_q2[j], od_lambda_k2[j], od_subln_g[j], od_w_o[j],
                                   lambda_init)
        h = rmsnorm(x, norm_ffn_g[i])
        x = x + conv_ffn(h, ffn_w_up[i], ffn_conv_w[i], ffn_conv_b[i], ffn_w_down[i])
    return rmsnorm(x, final_norm_g)
```

```python
import functools
import math

import jax
import jax.numpy as jnp
from jax import lax
from jax.experimental import pallas as pl
from jax.experimental.pallas import tpu as pltpu

EPS = 1e-6
NEG_INF = -1e30
V7X_VMEM_BYTES = 64 * 1024 * 1024
VMEM_LIMIT_BYTES = V7X_VMEM_BYTES - 8 * 1024 * 1024
SUBLANES = 8
LANES = 128
BF16_SUBLANES = 16

F32 = jnp.float32
BF16 = jnp.bfloat16


def _params(*sem):
    return pltpu.CompilerParams(dimension_semantics=sem, vmem_limit_bytes=VMEM_LIMIT_BYTES)


def _rms_rows(x, g):
    ms = jnp.mean(x * x, axis=-1, keepdims=True)
    return x * lax.rsqrt(ms + EPS) * g


def _layernorm_rows(x, g, b):
    mu = jnp.mean(x, axis=-1, keepdims=True)
    xc = x - mu
    var = jnp.mean(xc * xc, axis=-1, keepdims=True)
    return xc * lax.rsqrt(var + EPS) * g + b


def _norm_matmul_kernel(x_ref, g_ref, w_ref, cs_ref, o_ref, h_ref, *, row_chunk):
    @pl.when(pl.program_id(1) == 0)
    def _():
        def body(r, c):
            r0 = pl.multiple_of(r * row_chunk, row_chunk)
            h_ref[pl.ds(r0, row_chunk), :] = _rms_rows(
                x_ref[pl.ds(r0, row_chunk), :], g_ref[...]).astype(h_ref.dtype)
            return c
        lax.fori_loop(0, x_ref.shape[0] // row_chunk, body, 0)

    acc = jnp.dot(h_ref[...], w_ref[...], preferred_element_type=F32)
    o_ref[...] = (acc * cs_ref[...]).astype(o_ref.dtype)


def norm_matmul(x, g, w, col_scale, out_dtype, *, tm, tn):
    t, d = x.shape
    n = w.shape[1]
    return pl.pallas_call(
        functools.partial(_norm_matmul_kernel, row_chunk=min(tm, 128)),
        grid=(t // tm, n // tn),
        in_specs=[
            pl.BlockSpec((tm, d), lambda i, j: (i, 0)),
            pl.BlockSpec((1, d), lambda i, j: (0, 0)),
            pl.BlockSpec((d, tn), lambda i, j: (0, j)),
            pl.BlockSpec((1, tn), lambda i, j: (0, j)),
        ],
        out_specs=pl.BlockSpec((tm, tn), lambda i, j: (i, j)),
        out_shape=jax.ShapeDtypeStruct((t, n), out_dtype),
        scratch_shapes=[pltpu.VMEM((tm, d), BF16)],
        compiler_params=_params("parallel", "arbitrary"),
        name="norm_matmul",
    )(x, g, w, col_scale)


def _matmul_res_kernel(a_ref, w_ref, x_ref, o_ref):
    o_ref[...] = x_ref[...] + jnp.dot(a_ref[...], w_ref[...], preferred_element_type=F32)


def matmul_residual(a, w, x, *, tm, tn):
    t, k = a.shape
    n = w.shape[1]
    return pl.pallas_call(
        _matmul_res_kernel,
        grid=(t // tm, n // tn),
        in_specs=[
            pl.BlockSpec((tm, k), lambda i, j: (i, 0)),
            pl.BlockSpec((k, tn), lambda i, j: (0, j)),
            pl.BlockSpec((tm, tn), lambda i, j: (i, j)),
        ],
        out_specs=pl.BlockSpec((tm, tn), lambda i, j: (i, j)),
        out_shape=jax.ShapeDtypeStruct((t, n), F32),
        compiler_params=_params("parallel", "arbitrary"),
        name="matmul_residual",
    )(a, w, x)


FFN_HALO = BF16_SUBLANES


def _ffn_kernel(x_ref, xh_ref, g_ref, wg_ref, wv_ref, cwg_ref, cwv_ref, cbg_ref, cbv_ref,
                wd_ref, o_ref, h_ref, *, tiles_per_seq, row_chunk):
    i = pl.program_id(0)
    f = pl.program_id(1)
    tm = x_ref.shape[0]

    @pl.when(f == 0)
    def _():
        keep = ((i % tiles_per_seq) != 0).astype(F32)
        h_ref[0:FFN_HALO, :] = (_rms_rows(xh_ref[...], g_ref[...]) * keep).astype(h_ref.dtype)

        def body(r, c):
            r0 = pl.multiple_of(r * row_chunk, row_chunk)
            xr = x_ref[pl.ds(r0, row_chunk), :]
            h_ref[pl.ds(FFN_HALO + r0, row_chunk), :] = _rms_rows(xr, g_ref[...]).astype(h_ref.dtype)
            o_ref[pl.ds(r0, row_chunk), :] = xr
            return c
        lax.fori_loop(0, tm // row_chunk, body, 0)

    h = h_ref[...]

    def conv(z, cw_ref, cb_ref):
        z1 = pltpu.roll(z, 1, 0)
        z2 = pltpu.roll(z, 2, 0)
        y = cw_ref[2:3, :] * z + cw_ref[1:2, :] * z1 + cw_ref[0:1, :] * z2 + cb_ref[...]
        return y[FFN_HALO:, :]

    gate = conv(jnp.dot(h, wg_ref[...], preferred_element_type=F32), cwg_ref, cbg_ref)
    val = conv(jnp.dot(h, wv_ref[...], preferred_element_type=F32), cwv_ref, cbv_ref)
    act = (gate * jax.nn.sigmoid(gate) * val).astype(BF16)
    o_ref[...] += jnp.dot(act, wd_ref[...], preferred_element_type=F32)


def conv_ffn(x, g, w_up, conv_w, conv_b, w_down, *, seq, tm, tf):
    t, d = x.shape
    dff = w_down.shape[0]
    nf = dff // tf
    halo_blocks = tm // FFN_HALO
    return pl.pallas_call(
        functools.partial(_ffn_kernel, tiles_per_seq=seq // tm, row_chunk=min(tm, 128)),
        grid=(t // tm, nf),
        in_specs=[
            pl.BlockSpec((tm, d), lambda i, f: (i, 0)),
            pl.BlockSpec((FFN_HALO, d), lambda i, f: (jnp.maximum(i * halo_blocks - 1, 0), 0)),
            pl.BlockSpec((1, d), lambda i, f: (0, 0)),
            pl.BlockSpec((d, tf), lambda i, f: (0, f)),
            pl.BlockSpec((d, tf), lambda i, f: (0, f + nf)),
            pl.BlockSpec((3, tf), lambda i, f: (0, f)),
            pl.BlockSpec((3, tf), lambda i, f: (0, f + nf)),
            pl.BlockSpec((1, tf), lambda i, f: (0, f)),
            pl.BlockSpec((1, tf), lambda i, f: (0, f + nf)),
            pl.BlockSpec((tf, d), lambda i, f: (f, 0)),
        ],
        out_specs=pl.BlockSpec((tm, d), lambda i, f: (i, 0)),
        out_shape=jax.ShapeDtypeStruct((t, d), F32),
        scratch_shapes=[pltpu.VMEM((FFN_HALO + tm, d), BF16)],
        compiler_params=_params("parallel", "arbitrary"),
        name="conv_ffn",
    )(x, x, g, w_up, w_up, conv_w, conv_w, conv_b, conv_b, w_down)


MIX_HALO = 32


def _even_mix_kernel(av_ref, ag_ref, avh_ref, agh_ref, u_ref, v_ref, cw_ref, cb_ref,
                     lag_ref, lab_ref, lvg_ref, lvb_ref, ws_ref, bst_ref, o_ref,
                     a_scr, y_scr, v_scr, *, tiles_per_seq, width, row_chunk):
    i = pl.program_id(0)
    ts, c = av_ref.shape
    ext = ts + MIX_HALO
    heads, chunk, _ = ws_ref.shape

    keep = ((i % tiles_per_seq) != 0).astype(F32)
    a_scr[0, 0:MIX_HALO, :] = avh_ref[...] * jax.nn.sigmoid(agh_ref[...]) * keep
    a_scr[0, MIX_HALO:ext, :] = av_ref[...] * jax.nn.sigmoid(ag_ref[...])
    for r in range(1, SUBLANES):
        a_scr[r, 0:ext - SUBLANES, :] = a_scr[0, r:r + ext - SUBLANES, :]

    base = MIX_HALO - (width - 1)

    def conv_body(rc, carry):
        r0 = pl.multiple_of(rc * row_chunk, row_chunk)
        for l0 in range(0, c, LANES):
            acc = jnp.broadcast_to(cb_ref[:, l0:l0 + LANES], (row_chunk, LANES))
            for k in range(width):
                off = base + k
                acc = acc + cw_ref[k:k + 1, l0:l0 + LANES] * a_scr[
                    off % SUBLANES, pl.ds(r0 + (off // SUBLANES) * SUBLANES, row_chunk), l0:l0 + LANES]
            y_scr[pl.ds(r0, row_chunk), l0:l0 + LANES] = acc
        return carry
    lax.fori_loop(0, ts // row_chunk, conv_body, 0)

    def ln_body(rc, carry):
        r0 = pl.multiple_of(rc * row_chunk, row_chunk)
        ya = _layernorm_rows(y_scr[pl.ds(r0, row_chunk), :], lag_ref[...], lab_ref[...])
        o_ref[pl.ds(r0, row_chunk), 0:c] = (ya * jax.nn.sigmoid(ya)).astype(o_ref.dtype)
        vv = _layernorm_rows(jax.nn.gelu(v_ref[pl.ds(r0, row_chunk), :]), lvg_ref[...], lvb_ref[...])
        v_scr[pl.ds(r0, row_chunk), :] = vv.astype(v_scr.dtype)
        return carry
    lax.fori_loop(0, ts // row_chunk, ln_body, 0)

    hd = c // heads
    row = lax.broadcasted_iota(jnp.int32, (chunk, chunk), 0)
    col = lax.broadcasted_iota(jnp.int32, (chunk, chunk), 1)
    for hh in range(heads):
        wm = jnp.where(col <= row, ws_ref[hh], 0.0).astype(BF16)
        bias = bst_ref[:, hh:hh + 1]
        for cc in range(ts // chunk):
            rows = slice(cc * chunk, (cc + 1) * chunk)
            cols = slice(hh * hd, (hh + 1) * hd)
            s = jnp.dot(wm, v_scr[rows, cols], preferred_element_type=F32) + bias
            o_ref[rows, c + hh * hd:c + (hh + 1) * hd] = (
                jax.nn.gelu(u_ref[rows, cols]) * s).astype(o_ref.dtype)


def even_mix(z, conv_w, conv_b, ln_a_g, ln_a_b, ln_v_g, ln_v_b, w_s, b_s_t, *, seq, ts):
    t = z.shape[0]
    c = z.shape[1] // 4
    width = conv_w.shape[0]
    heads, chunk, _ = w_s.shape
    assert width - 1 <= MIX_HALO and ts % chunk == 0 and seq % ts == 0
    halo_blocks = ts // MIX_HALO
    row = lambda i: (i, 0)
    vec = pl.BlockSpec((1, c), lambda i: (0, 0))
    return pl.pallas_call(
        functools.partial(_even_mix_kernel, tiles_per_seq=seq // ts, width=width, row_chunk=32),
        grid=(t // ts,),
        in_specs=[
            pl.BlockSpec((ts, c), lambda i: (i, 0)),
            pl.BlockSpec((ts, c), lambda i: (i, 1)),
            pl.BlockSpec((MIX_HALO, c), lambda i: (jnp.maximum(i * halo_blocks - 1, 0), 0)),
            pl.BlockSpec((MIX_HALO, c), lambda i: (jnp.maximum(i * halo_blocks - 1, 0), 1)),
            pl.BlockSpec((ts, c), lambda i: (i, 2)),
            pl.BlockSpec((ts, c), lambda i: (i, 3)),
            pl.BlockSpec((width, c), lambda i: (0, 0)),
            vec, vec, vec, vec, vec,
            pl.BlockSpec((heads, chunk, chunk), lambda i: (0, 0, 0)),
            pl.BlockSpec((chunk, heads), lambda i: (0, 0)),
        ],
        out_specs=pl.BlockSpec((ts, 2 * c), row),
        out_shape=jax.ShapeDtypeStruct((t, 2 * c), BF16),
        scratch_shapes=[
            pltpu.VMEM((SUBLANES, ts + MIX_HALO, c), F32),
            pltpu.VMEM((ts, c), F32),
            pltpu.VMEM((ts, c), BF16),
        ],
        compiler_params=_params("parallel"),
        name="even_mix",
    )(z, z, z, z, z, z, conv_w, conv_b, ln_a_g, ln_a_b, ln_v_g, ln_v_b, w_s, b_s_t)


def _diff_attn_kernel(q_ref, k_ref, v_ref, lq1_ref, lk1_ref, lq2_ref, lk2_ref, sg_ref, o_ref,
                      m_scr, l_scr, acc_scr, *, lambda_init, tk):
    i = pl.program_id(2)
    tq, two_dh = q_ref.shape
    dh = two_dh // 2
    assert tq == tk

    m_scr[...] = jnp.full(m_scr.shape, NEG_INF, F32)
    l_scr[...] = jnp.zeros(l_scr.shape, F32)
    acc_scr[...] = jnp.zeros(acc_scr.shape, F32)

    def step(j, masked):
        k0 = pl.multiple_of(j * tk, tk)
        kblk = k_ref[pl.ds(k0, tk), :]
        vblk = v_ref[pl.ds(k0, tk), :]
        for c in range(2):
            s = lax.dot_general(q_ref[:, c * dh:(c + 1) * dh], kblk[:, c * dh:(c + 1) * dh],
                                (((1,), (1,)), ((), ())), preferred_element_type=F32)
            if masked:
                row = lax.broadcasted_iota(jnp.int32, (tq, tk), 0)
                col = lax.broadcasted_iota(jnp.int32, (tq, tk), 1)
                s = jnp.where(col <= row, s, NEG_INF)
            m_prev = m_scr[c]
            m_new = jnp.maximum(m_prev, jnp.max(s, axis=-1, keepdims=True))
            alpha = jnp.exp(m_prev - m_new)
            p = jnp.exp(s - m_new)
            l_scr[c] = alpha * l_scr[c] + jnp.sum(p, axis=-1, keepdims=True)
            acc_scr[c] = alpha * acc_scr[c] + jnp.dot(p.astype(BF16), vblk, preferred_element_type=F32)
            m_scr[c] = m_new

    def body(j, carry):
        step(j, False)
        return carry
    lax.fori_loop(0, i, body, 0)
    step(i, True)

    lam = (jnp.exp(jnp.sum(lq1_ref[...] * lk1_ref[...], axis=-1, keepdims=True))
           - jnp.exp(jnp.sum(lq2_ref[...] * lk2_ref[...], axis=-1, keepdims=True))
           + lambda_init)
    o = acc_scr[0] / l_scr[0] - lam * (acc_scr[1] / l_scr[1])
    o_ref[...] = (_rms_rows(o, sg_ref[...]) * (1.0 - lambda_init)).astype(o_ref.dtype)


def diff_attention(qkv, lq1, lk1, lq2, lk2, subln_g, *, batch, seq, heads, lambda_init, tq):
    t = qkv.shape[0]
    dv = subln_g.shape[1]
    nq = seq // tq
    vec = lambda n: pl.BlockSpec((1, n), lambda b, h, i: (0, 0))
    dh = lq1.shape[1]
    return pl.pallas_call(
        functools.partial(_diff_attn_kernel, lambda_init=lambda_init, tk=tq),
        grid=(batch, heads, nq),
        in_specs=[
            pl.BlockSpec((tq, dv), lambda b, h, i: (b * nq + i, h)),
            pl.BlockSpec((seq, dv), lambda b, h, i: (b, heads + h)),
            pl.BlockSpec((seq, dv), lambda b, h, i: (b, 2 * heads + h)),
            vec(dh), vec(dh), vec(dh), vec(dh), vec(dv),
        ],
        out_specs=pl.BlockSpec((tq, dv), lambda b, h, i: (b * nq + i, h)),
        out_shape=jax.ShapeDtypeStruct((t, heads * dv), BF16),
        scratch_shapes=[
            pltpu.VMEM((2, tq, 1), F32),
            pltpu.VMEM((2, tq, 1), F32),
            pltpu.VMEM((2, tq, dv), F32),
        ],
        compiler_params=_params("parallel", "parallel", "arbitrary"),
        name="diff_attention",
    )(qkv, qkv, qkv, lq1, lk1, lq2, lk2, subln_g)


def _rmsnorm_kernel(x_ref, g_ref, o_ref):
    o_ref[...] = _rms_rows(x_ref[...], g_ref[...])


def rmsnorm(x, g, *, tm):
    t, d = x.shape
    return pl.pallas_call(
        _rmsnorm_kernel,
        grid=(t // tm,),
        in_specs=[pl.BlockSpec((tm, d), lambda i: (i, 0)), pl.BlockSpec((1, d), lambda i: (0, 0))],
        out_specs=pl.BlockSpec((tm, d), lambda i: (i, 0)),
        out_shape=jax.ShapeDtypeStruct((t, d), F32),
        compiler_params=_params("parallel"),
        name="final_rmsnorm",
    )(x, g)


def _tiles(seq, d_ff):
    return dict(
        proj_tm=min(1024, seq), proj_tn=1024,
        ffn_tm=min(512, seq), ffn_tf=512 if d_ff % 512 == 0 else 128,
        mix_ts=min(256, seq), attn_tq=min(512, seq), norm_tm=min(256, seq),
    )


def _forward(x, norm_mix_g, norm_ffn_g, final_norm_g, ev_w_in, ev_conv_w, ev_conv_b, ev_ln_a_g,
             ev_ln_a_b, ev_ln_v_g, ev_ln_v_b, ev_w_s, ev_b_s, ev_w_out, od_w_qkv, od_lambda_q1,
             od_lambda_k1, od_lambda_q2, od_lambda_k2, od_subln_g, od_w_o, ffn_w_up, ffn_conv_w,
             ffn_conv_b, ffn_w_down, tiles=None):
    batch, seq, d = x.shape
    depth = norm_mix_g.shape[0]
    dh = od_lambda_q1.shape[-1]
    dv = od_subln_g.shape[-1]
    heads = od_w_o.shape[1] // dv
    qk_w = heads * 2 * dh
    tl = dict(_tiles(seq, ffn_w_down.shape[1]))
    if tiles:
        tl.update(tiles)

    row = lambda v: v.reshape(1, -1)
    xt = x.reshape(batch * seq, d)
    for i in range(depth):
        j = i // 2
        if i % 2 == 0:
            w_in = ev_w_in[j].astype(BF16)
            ones = jnp.ones((1, w_in.shape[1]), F32)
            z = norm_matmul(xt, row(norm_mix_g[i]), w_in, ones, F32,
                            tm=tl["proj_tm"], tn=tl["proj_tn"])
            cat = even_mix(z, ev_conv_w[j], row(ev_conv_b[j]), row(ev_ln_a_g[j]), row(ev_ln_a_b[j]),
                           row(ev_ln_v_g[j]), row(ev_ln_v_b[j]), ev_w_s[j], ev_b_s[j].T,
                           seq=seq, ts=tl["mix_ts"])
            xt = matmul_residual(cat, ev_w_out[j].astype(BF16), xt, tm=tl["proj_tm"], tn=tl["proj_tn"])
        else:
            lambda_init = 0.8 - 0.6 * math.exp(-0.3 * i)
            w_qkv = od_w_qkv[j].astype(BF16)
            col_scale = jnp.concatenate(
                [jnp.full((1, qk_w), 1.0 / math.sqrt(dh), F32),
                 jnp.ones((1, w_qkv.shape[1] - qk_w), F32)], axis=1)
            qkv = norm_matmul(xt, row(norm_mix_g[i]), w_qkv, col_scale, BF16,
                              tm=tl["proj_tm"], tn=tl["proj_tn"])
            o = diff_attention(qkv, row(od_lambda_q1[j]), row(od_lambda_k1[j]), row(od_lambda_q2[j]),
                               row(od_lambda_k2[j]), row(od_subln_g[j]), batch=batch, seq=seq,
                               heads=heads, lambda_init=lambda_init, tq=tl["attn_tq"])
            xt = matmul_residual(o, od_w_o[j].astype(BF16), xt, tm=tl["proj_tm"], tn=tl["proj_tn"])
        xt = conv_ffn(xt, row(norm_ffn_g[i]), ffn_w_up[i].astype(BF16), ffn_conv_w[i],
                      row(ffn_conv_b[i]), ffn_w_down[i].astype(BF16),
                      seq=seq, tm=tl["ffn_tm"], tf=tl["ffn_tf"])
    out = rmsnorm(xt, row(final_norm_g), tm=tl["norm_tm"])
    return out.reshape(batch, seq, d)


def kernel(x, norm_mix_g, norm_ffn_g, final_norm_g, ev_w_in, ev_conv_w, ev_conv_b, ev_ln_a_g, ev_ln_a_b, ev_ln_v_g, ev_ln_v_b, ev_w_s, ev_b_s, ev_w_out, od_w_qkv, od_lambda_q1, od_lambda_k1, od_lambda_q2, od_lambda_k2, od_subln_g, od_w_o, ffn_w_up, ffn_conv_w, ffn_conv_b, ffn_w_down):
    return _forward(x, norm_mix_g, norm_ffn_g, final_norm_g, ev_w_in, ev_conv_w, ev_conv_b, ev_ln_a_g,
                    ev_ln_a_b, ev_ln_v_g, ev_ln_v_b, ev_w_s, ev_b_s, ev_w_out, od_w_qkv, od_lambda_q1,
                    od_lambda_k1, od_lambda_q2, od_lambda_k2, od_subln_g, od_w_o, ffn_w_up,
                    ffn_conv_w, ffn_conv_b, ffn_w_down)
```

```python
import functools
import math

import jax
import jax.numpy as jnp
from jax import lax
from jax.experimental import pallas as pl
from jax.experimental.pallas import tpu as pltpu

EPS = 1e-6
NEG_INF = -1e30
V7X_VMEM_BYTES = 64 * 1024 * 1024
VMEM_LIMIT_BYTES = V7X_VMEM_BYTES - 8 * 1024 * 1024
SUBLANES = 8
LANES = 128
BF16_SUBLANES = 16

F32 = jnp.float32
BF16 = jnp.bfloat16


def _params(*sem):
    return pltpu.CompilerParams(dimension_semantics=sem, vmem_limit_bytes=VMEM_LIMIT_BYTES)


def _rms_rows(x, g):
    ms = jnp.mean(x * x, axis=-1, keepdims=True)
    return x * lax.rsqrt(ms + EPS) * g


def _layernorm_rows(x, g, b):
    mu = jnp.mean(x, axis=-1, keepdims=True)
    xc = x - mu
    var = jnp.mean(xc * xc, axis=-1, keepdims=True)
    return xc * lax.rsqrt(var + EPS) * g + b


def _norm_matmul_kernel(x_ref, g_ref, w_ref, cs_ref, o_ref, h_ref, *, row_chunk):
    @pl.when(pl.program_id(1) == 0)
    def _():
        def body(r, c):
            r0 = pl.multiple_of(r * row_chunk, row_chunk)
            h_ref[pl.ds(r0, row_chunk), :] = _rms_rows(
                x_ref[pl.ds(r0, row_chunk), :], g_ref[...]).astype(h_ref.dtype)
            return c
        lax.fori_loop(0, x_ref.shape[0] // row_chunk, body, 0)

    acc = jnp.dot(h_ref[...], w_ref[...], preferred_element_type=F32)
    o_ref[...] = (acc * cs_ref[...]).astype(o_ref.dtype)


def norm_matmul(x, g, w, col_scale, out_dtype, *, tm, tn):
    t, d = x.shape
    n = w.shape[1]
    return pl.pallas_call(
        functools.partial(_norm_matmul_kernel, row_chunk=min(tm, 128)),
        grid=(t // tm, n // tn),
        in_specs=[
            pl.BlockSpec((tm, d), lambda i, j: (i, 0)),
            pl.BlockSpec((1, d), lambda i, j: (0, 0)),
            pl.BlockSpec((d, tn), lambda i, j: (0, j)),
            pl.BlockSpec((1, tn), lambda i, j: (0, j)),
        ],
        out_specs=pl.BlockSpec((tm, tn), lambda i, j: (i, j)),
        out_shape=jax.ShapeDtypeStruct((t, n), out_dtype),
        scratch_shapes=[pltpu.VMEM((tm, d), BF16)],
        compiler_params=_params("parallel", "arbitrary"),
        name="norm_matmul",
    )(x, g, w, col_scale)


def _matmul_res_kernel(a_ref, w_ref, x_ref, o_ref):
    o_ref[...] = x_ref[...] + jnp.dot(a_ref[...], w_ref[...], preferred_element_type=F32)


def matmul_residual(a, w, x, *, tm, tn):
    t, k = a.shape
    n = w.shape[1]
    return pl.pallas_call(
        _matmul_res_kernel,
        grid=(t // tm, n // tn),
        in_specs=[
            pl.BlockSpec((tm, k), lambda i, j: (i, 0)),
            pl.BlockSpec((k, tn), lambda i, j: (0, j)),
            pl.BlockSpec((tm, tn), lambda i, j: (i, j)),
        ],
        out_specs=pl.BlockSpec((tm, tn), lambda i, j: (i, j)),
        out_shape=jax.ShapeDtypeStruct((t, n), F32),
        compiler_params=_params("parallel", "arbitrary"),
        name="matmul_residual",
    )(a, w, x)


FFN_HALO = BF16_SUBLANES


def _ffn_kernel(x_ref, xh_ref, g_ref, wg_ref, wv_ref, cwg_ref, cwv_ref, cbg_ref, cbv_ref,
                wd_ref, o_ref, h_ref, *, tiles_per_seq, row_chunk):
    i = pl.program_id(0)
    f = pl.program_id(1)
    tm = x_ref.shape[0]

    @pl.when(f == 0)
    def _():
        keep = ((i % tiles_per_seq) != 0).astype(F32)
        h_ref[0:FFN_HALO, :] = (_rms_rows(xh_ref[...], g_ref[...]) * keep).astype(h_ref.dtype)

        def body(r, c):
            r0 = pl.multiple_of(r * row_chunk, row_chunk)
            xr = x_ref[pl.ds(r0, row_chunk), :]
            h_ref[pl.ds(FFN_HALO + r0, row_chunk), :] = _rms_rows(xr, g_ref[...]).astype(h_ref.dtype)
            o_ref[pl.ds(r0, row_chunk), :] = xr
            return c
        lax.fori_loop(0, tm // row_chunk, body, 0)

    h = h_ref[...]

    def conv(z, cw_ref, cb_ref):
        z1 = pltpu.roll(z, 1, 0)
        z2 = pltpu.roll(z, 2, 0)
        y = cw_ref[2:3, :] * z + cw_ref[1:2, :] * z1 + cw_ref[0:1, :] * z2 + cb_ref[...]
        return y[FFN_HALO:, :]

    gate = conv(jnp.dot(h, wg_ref[...], preferred_element_type=F32), cwg_ref, cbg_ref)
    val = conv(jnp.dot(h, wv_ref[...], preferred_element_type=F32), cwv_ref, cbv_ref)
    act = (gate * jax.nn.sigmoid(gate) * val).astype(BF16)
    o_ref[...] += jnp.dot(act, wd_ref[...], preferred_element_type=F32)


def conv_ffn(x, g, w_up, conv_w, conv_b, w_down, *, seq, tm, tf):
    t, d = x.shape
    dff = w_down.shape[0]
    nf = dff // tf
    halo_blocks = tm // FFN_HALO
    return pl.pallas_call(
        functools.partial(_ffn_kernel, tiles_per_seq=seq // tm, row_chunk=min(tm, 128)),
        grid=(t // tm, nf),
        in_specs=[
            pl.BlockSpec((tm, d), lambda i, f: (i, 0)),
            pl.BlockSpec((FFN_HALO, d), lambda i, f: (jnp.maximum(i * halo_blocks - 1, 0), 0)),
            pl.BlockSpec((1, d), lambda i, f: (0, 0)),
            pl.BlockSpec((d, tf), lambda i, f: (0, f)),
            pl.BlockSpec((d, tf), lambda i, f: (0, f + nf)),
            pl.BlockSpec((3, tf), lambda i, f: (0, f)),
            pl.BlockSpec((3, tf), lambda i, f: (0, f + nf)),
            pl.BlockSpec((1, tf), lambda i, f: (0, f)),
            pl.BlockSpec((1, tf), lambda i, f: (0, f + nf)),
            pl.BlockSpec((tf, d), lambda i, f: (f, 0)),
        ],
        out_specs=pl.BlockSpec((tm, d), lambda i, f: (i, 0)),
        out_shape=jax.ShapeDtypeStruct((t, d), F32),
        scratch_shapes=[pltpu.VMEM((FFN_HALO + tm, d), BF16)],
        compiler_params=_params("parallel", "arbitrary"),
        name="conv_ffn",
    )(x, x, g, w_up, w_up, conv_w, conv_w, conv_b, conv_b, w_down)


MIX_HALO = 32


def _even_mix_kernel(av_ref, ag_ref, avh_ref, agh_ref, u_ref, v_ref, cw_ref, cb_ref,
                     lag_ref, lab_ref, lvg_ref, lvb_ref, ws_ref, bst_ref, o_ref,
                     a_scr, y_scr, v_scr, *, tiles_per_seq, width, row_chunk):
    i = pl.program_id(0)
    ts, c = av_ref.shape
    ext = ts + MIX_HALO
    heads, chunk, _ = ws_ref.shape

    keep = ((i % tiles_per_seq) != 0).astype(F32)
    a_scr[0, 0:MIX_HALO, :] = avh_ref[...] * jax.nn.sigmoid(agh_ref[...]) * keep
    a_scr[0, MIX_HALO:ext, :] = av_ref[...] * jax.nn.sigmoid(ag_ref[...])
    for r in range(1, SUBLANES):
        a_scr[r, 0:ext - SUBLANES, :] = a_scr[0, r:r + ext - SUBLANES, :]

    base = MIX_HALO - (width - 1)

    def conv_body(rc, carry):
        r0 = pl.multiple_of(rc * row_chunk, row_chunk)
        for l0 in range(0, c, LANES):
            acc = jnp.broadcast_to(cb_ref[:, l0:l0 + LANES], (row_chunk, LANES))
            for k in range(width):
                off = base + k
                acc = acc + cw_ref[k:k + 1, l0:l0 + LANES] * a_scr[
                    off % SUBLANES, pl.ds(r0 + (off // SUBLANES) * SUBLANES, row_chunk), l0:l0 + LANES]
            y_scr[pl.ds(r0, row_chunk), l0:l0 + LANES] = acc
        return carry
    lax.fori_loop(0, ts // row_chunk, conv_body, 0)

    def ln_body(rc, carry):
        r0 = pl.multiple_of(rc * row_chunk, row_chunk)
        ya = _layernorm_rows(y_scr[pl.ds(r0, row_chunk), :], lag_ref[...], lab_ref[...])
        o_ref[pl.ds(r0, row_chunk), 0:c] = (ya * jax.nn.sigmoid(ya)).astype(o_ref.dtype)
        vv = _layernorm_rows(jax.nn.gelu(v_ref[pl.ds(r0, row_chunk), :]), lvg_ref[...], lvb_ref[...])
        v_scr[pl.ds(r0, row_chunk), :] = vv.astype(v_scr.dtype)
        return carry
    lax.fori_loop(0, ts // row_chunk, ln_body, 0)

    hd = c // heads
    row = lax.broadcasted_iota(jnp.int32, (chunk, chunk), 0)
    col = lax.broadcasted_iota(jnp.int32, (chunk, chunk), 1)
    for hh in range(heads):
        wm = jnp.where(col <= row, ws_ref[hh], 0.0).astype(BF16)
        bias = bst_ref[:, hh:hh + 1]
        for cc in range(ts // chunk):
            rows = slice(cc * chunk, (cc + 1) * chunk)
            cols = slice(hh * hd, (hh + 1) * hd)
            s = jnp.dot(wm, v_scr[rows, cols], preferred_element_type=F32) + bias
            o_ref[rows, c + hh * hd:c + (hh + 1) * hd] = (
                jax.nn.gelu(u_ref[rows, cols]) * s).astype(o_ref.dtype)


def even_mix(z, conv_w, conv_b, ln_a_g, ln_a_b, ln_v_g, ln_v_b, w_s, b_s_t, *, seq, ts):
    t = z.shape[0]
    c = z.shape[1] // 4
    width = conv_w.shape[0]
    heads, chunk, _ = w_s.shape
    assert width - 1 <= MIX_HALO and ts % chunk == 0 and seq % ts == 0
    halo_blocks = ts // MIX_HALO
    row = lambda i: (i, 0)
    vec = pl.BlockSpec((1, c), lambda i: (0, 0))
    return pl.pallas_call(
        functools.partial(_even_mix_kernel, tiles_per_seq=seq // ts, width=width, row_chunk=32),
        grid=(t // ts,),
        in_specs=[
            pl.BlockSpec((ts, c), lambda i: (i, 0)),
            pl.BlockSpec((ts, c), lambda i: (i, 1)),
            pl.BlockSpec((MIX_HALO, c), lambda i: (jnp.maximum(i * halo_blocks - 1, 0), 0)),
            pl.BlockSpec((MIX_HALO, c), lambda i: (jnp.maximum(i * halo_blocks - 1, 0), 1)),
            pl.BlockSpec((ts, c), lambda i: (i, 2)),
            pl.BlockSpec((ts, c), lambda i: (i, 3)),
            pl.BlockSpec((width, c), lambda i: (0, 0)),
            vec, vec, vec, vec, vec,
            pl.BlockSpec((heads, chunk, chunk), lambda i: (0, 0, 0)),
            pl.BlockSpec((chunk, heads), lambda i: (0, 0)),
        ],
        out_specs=pl.BlockSpec((ts, 2 * c), row),
        out_shape=jax.ShapeDtypeStruct((t, 2 * c), BF16),
        scratch_shapes=[
            pltpu.VMEM((SUBLANES, ts + MIX_HALO, c), F32),
            pltpu.VMEM((ts, c), F32),
            pltpu.VMEM((ts, c), BF16),
        ],
        compiler_params=_params("parallel"),
        name="even_mix",
    )(z, z, z, z, z, z, conv_w, conv_b, ln_a_g, ln_a_b, ln_v_g, ln_v_b, w_s, b_s_t)


def _diff_attn_kernel(q_ref, k_ref, v_ref, lq1_ref, lk1_ref, lq2_ref, lk2_ref, sg_ref, o_ref,
                      m_scr, l_scr, acc_scr, *, lambda_init, tk):
    i = pl.program_id(2)
    tq, dv = q_ref.shape
    dh = dv // 2
    lane_tiles = tk // LANES

    m_scr[...] = jnp.full(m_scr.shape, NEG_INF, F32)
    l_scr[...] = jnp.zeros(l_scr.shape, F32)
    acc_scr[...] = jnp.zeros(acc_scr.shape, F32)

    def step(j, r0, diagonal):
        k0 = pl.multiple_of(j * tk, tk)
        kblk = k_ref[pl.ds(k0, tk), :]
        vblk = v_ref[pl.ds(k0, tk), :]
        for c in range(2):
            s = lax.dot_general(q_ref[r0:, c * dh:(c + 1) * dh], kblk[:, c * dh:(c + 1) * dh],
                                (((1,), (1,)), ((), ())), preferred_element_type=F32)
            if diagonal:
                row = lax.broadcasted_iota(jnp.int32, (tk, tk), 0)
                col = lax.broadcasted_iota(jnp.int32, (tk, tk), 1)
                top = jnp.where(col <= row, s[:tk], NEG_INF)
                s = top if r0 + tk == tq else jnp.concatenate([top, s[tk:]], axis=0)
            tiles = [s[:, t * LANES:(t + 1) * LANES] for t in range(lane_tiles)]
            tile_max = functools.reduce(jnp.maximum, tiles)
            m_prev = m_scr[c, r0:, :]
            m_new = jnp.maximum(m_prev, jnp.max(tile_max, axis=-1, keepdims=True))
            alpha = jnp.exp2(m_prev - m_new)
            p_tiles = [jnp.exp2(t - m_new) for t in tiles]
            l_scr[c, r0:, :] = alpha * l_scr[c, r0:, :] + functools.reduce(jnp.add, p_tiles)
            p = jnp.concatenate(p_tiles, axis=1).astype(BF16)
            alpha_v = jnp.concatenate([alpha] * (dv // LANES), axis=1)
            acc_scr[c, r0:, :] = alpha_v * acc_scr[c, r0:, :] + jnp.dot(
                p, vblk, preferred_element_type=F32)
            m_scr[c, r0:, :] = m_new

    n_full = i * (tq // tk)

    def body(j, carry):
        step(j, 0, False)
        return carry
    lax.fori_loop(0, n_full, body, 0)
    for d in range(tq // tk):
        step(n_full + d, d * tk, True)

    lam = (jnp.exp(jnp.sum(lq1_ref[...] * lk1_ref[...], axis=-1, keepdims=True))
           - jnp.exp(jnp.sum(lq2_ref[...] * lk2_ref[...], axis=-1, keepdims=True))
           + lambda_init)
    l1 = jnp.sum(l_scr[0], axis=-1, keepdims=True)
    l2 = jnp.sum(l_scr[1], axis=-1, keepdims=True)
    o = acc_scr[0] / l1 - lam * (acc_scr[1] / l2)
    o_ref[...] = (_rms_rows(o, sg_ref[...]) * (1.0 - lambda_init)).astype(o_ref.dtype)


def diff_attention(qkv, lq1, lk1, lq2, lk2, subln_g, *, batch, seq, heads, lambda_init, tq, tk):
    t = qkv.shape[0]
    dv = subln_g.shape[1]
    dh = lq1.shape[1]
    assert dv == 2 * dh and tq % tk == 0 and tk % LANES == 0
    nq = seq // tq
    vec = lambda n: pl.BlockSpec((1, n), lambda b, h, i: (0, 0))
    return pl.pallas_call(
        functools.partial(_diff_attn_kernel, lambda_init=lambda_init, tk=tk),
        grid=(batch, heads, nq),
        in_specs=[
            pl.BlockSpec((tq, dv), lambda b, h, i: (b * nq + i, h)),
            pl.BlockSpec((seq, dv), lambda b, h, i: (b, heads + h)),
            pl.BlockSpec((seq, dv), lambda b, h, i: (b, 2 * heads + h)),
            vec(dh), vec(dh), vec(dh), vec(dh), vec(dv),
        ],
        out_specs=pl.BlockSpec((tq, dv), lambda b, h, i: (b * nq + i, h)),
        out_shape=jax.ShapeDtypeStruct((t, heads * dv), BF16),
        scratch_shapes=[
            pltpu.VMEM((2, tq, LANES), F32),
            pltpu.VMEM((2, tq, LANES), F32),
            pltpu.VMEM((2, tq, dv), F32),
        ],
        compiler_params=_params("parallel", "parallel", "arbitrary"),
        name="diff_attention",
    )(qkv, qkv, qkv, lq1, lk1, lq2, lk2, subln_g)


def _rmsnorm_kernel(x_ref, g_ref, o_ref):
    o_ref[...] = _rms_rows(x_ref[...], g_ref[...])


def rmsnorm(x, g, *, tm):
    t, d = x.shape
    return pl.pallas_call(
        _rmsnorm_kernel,
        grid=(t // tm,),
        in_specs=[pl.BlockSpec((tm, d), lambda i: (i, 0)), pl.BlockSpec((1, d), lambda i: (0, 0))],
        out_specs=pl.BlockSpec((tm, d), lambda i: (i, 0)),
        out_shape=jax.ShapeDtypeStruct((t, d), F32),
        compiler_params=_params("parallel"),
        name="final_rmsnorm",
    )(x, g)


def _tiles(seq, d_ff):
    return dict(
        proj_tm=min(1024, seq), proj_tn=1024,
        ffn_tm=min(512, seq), ffn_tf=512 if d_ff % 512 == 0 else 128,
        mix_ts=min(256, seq), attn_tq=min(2048, seq), attn_tk=min(512, seq), norm_tm=min(256, seq),
    )


def _forward(x, norm_mix_g, norm_ffn_g, final_norm_g, ev_w_in, ev_conv_w, ev_conv_b, ev_ln_a_g,
             ev_ln_a_b, ev_ln_v_g, ev_ln_v_b, ev_w_s, ev_b_s, ev_w_out, od_w_qkv, od_lambda_q1,
             od_lambda_k1, od_lambda_q2, od_lambda_k2, od_subln_g, od_w_o, ffn_w_up, ffn_conv_w,
             ffn_conv_b, ffn_w_down, tiles=None):
    batch, seq, d = x.shape
    depth = norm_mix_g.shape[0]
    dh = od_lambda_q1.shape[-1]
    dv = od_subln_g.shape[-1]
    heads = od_w_o.shape[1] // dv
    qk_w = heads * 2 * dh
    tl = dict(_tiles(seq, ffn_w_down.shape[1]))
    if tiles:
        tl.update(tiles)

    row = lambda v: v.reshape(1, -1)
    xt = x.reshape(batch * seq, d)
    for i in range(depth):
        j = i // 2
        if i % 2 == 0:
            w_in = ev_w_in[j].astype(BF16)
            ones = jnp.ones((1, w_in.shape[1]), F32)
            z = norm_matmul(xt, row(norm_mix_g[i]), w_in, ones, F32,
                            tm=tl["proj_tm"], tn=tl["proj_tn"])
            cat = even_mix(z, ev_conv_w[j], row(ev_conv_b[j]), row(ev_ln_a_g[j]), row(ev_ln_a_b[j]),
                           row(ev_ln_v_g[j]), row(ev_ln_v_b[j]), ev_w_s[j], ev_b_s[j].T,
                           seq=seq, ts=tl["mix_ts"])
            xt = matmul_residual(cat, ev_w_out[j].astype(BF16), xt, tm=tl["proj_tm"], tn=tl["proj_tn"])
        else:
            lambda_init = 0.8 - 0.6 * math.exp(-0.3 * i)
            w_qkv = od_w_qkv[j].astype(BF16)
            col_scale = jnp.concatenate(
                [jnp.full((1, qk_w), math.log2(math.e) / math.sqrt(dh), F32),
                 jnp.ones((1, w_qkv.shape[1] - qk_w), F32)], axis=1)
            qkv = norm_matmul(xt, row(norm_mix_g[i]), w_qkv, col_scale, BF16,
                              tm=tl["proj_tm"], tn=tl["proj_tn"])
            o = diff_attention(qkv, row(od_lambda_q1[j]), row(od_lambda_k1[j]), row(od_lambda_q2[j]),
                               row(od_lambda_k2[j]), row(od_subln_g[j]), batch=batch, seq=seq,
                               heads=heads, lambda_init=lambda_init, tq=tl["attn_tq"], tk=tl["attn_tk"])
            xt = matmul_residual(o, od_w_o[j].astype(BF16), xt, tm=tl["proj_tm"], tn=tl["proj_tn"])
        xt = conv_ffn(xt, row(norm_ffn_g[i]), ffn_w_up[i].astype(BF16), ffn_conv_w[i],
                      row(ffn_conv_b[i]), ffn_w_down[i].astype(BF16),
                      seq=seq, tm=tl["ffn_tm"], tf=tl["ffn_tf"])
    out = rmsnorm(xt, row(final_norm_g), tm=tl["norm_tm"])
    return out.reshape(batch, seq, d)


def kernel(x, norm_mix_g, norm_ffn_g, final_norm_g, ev_w_in, ev_conv_w, ev_conv_b, ev_ln_a_g, ev_ln_a_b, ev_ln_v_g, ev_ln_v_b, ev_w_s, ev_b_s, ev_w_out, od_w_qkv, od_lambda_q1, od_lambda_k1, od_lambda_q2, od_lambda_k2, od_subln_g, od_w_o, ffn_w_up, ffn_conv_w, ffn_conv_b, ffn_w_down):
    return _forward(x, norm_mix_g, norm_ffn_g, final_norm_g, ev_w_in, ev_conv_w, ev_conv_b, ev_ln_a_g,
                    ev_ln_a_b, ev_ln_v_g, ev_ln_v_b, ev_w_s, ev_b_s, ev_w_out, od_w_qkv, od_lambda_q1,
                    od_lambda_k1, od_lambda_q2, od_lambda_k2, od_subln_g, od_w_o, ffn_w_up,
                    ffn_conv_w, ffn_conv_b, ffn_w_down)
```

```python
import functools
import math

import jax
import jax.numpy as jnp
from jax import lax
from jax.experimental import pallas as pl
from jax.experimental.pallas import tpu as pltpu

EPS = 1e-6
NEG_INF = -1e30
V7X_VMEM_BYTES = 64 * 1024 * 1024
VMEM_LIMIT_BYTES = V7X_VMEM_BYTES - 8 * 1024 * 1024
SUBLANES = 8
LANES = 128
BF16_SUBLANES = 16
MXU_COLS = 256

F32 = jnp.float32
BF16 = jnp.bfloat16


def _params(*sem):
    return pltpu.CompilerParams(dimension_semantics=sem, vmem_limit_bytes=VMEM_LIMIT_BYTES)


def _rms_rows(x, g):
    ms = jnp.mean(x * x, axis=-1, keepdims=True)
    return x * lax.rsqrt(ms + EPS) * g


def _layernorm_rows(x, g, b):
    mu = jnp.mean(x, axis=-1, keepdims=True)
    xc = x - mu
    var = jnp.mean(xc * xc, axis=-1, keepdims=True)
    return xc * lax.rsqrt(var + EPS) * g + b


def _norm_matmul_kernel(x_ref, g_ref, w_ref, cs_ref, o_ref, h_ref, *, row_chunk):
    @pl.when(pl.program_id(1) == 0)
    def _():
        def body(r, c):
            r0 = pl.multiple_of(r * row_chunk, row_chunk)
            h_ref[pl.ds(r0, row_chunk), :] = _rms_rows(
                x_ref[pl.ds(r0, row_chunk), :], g_ref[...]).astype(h_ref.dtype)
            return c
        lax.fori_loop(0, x_ref.shape[0] // row_chunk, body, 0)

    acc = jnp.dot(h_ref[...], w_ref[...], preferred_element_type=F32)
    o_ref[...] = (acc * cs_ref[...]).astype(o_ref.dtype)


def norm_matmul(x, g, w, layer, col_scale, out_dtype, *, tm, tn):
    t, d = x.shape
    n = w.shape[2]
    return pl.pallas_call(
        functools.partial(_norm_matmul_kernel, row_chunk=min(tm, 128)),
        grid=(t // tm, n // tn),
        in_specs=[
            pl.BlockSpec((tm, d), lambda i, j: (i, 0)),
            pl.BlockSpec((1, d), lambda i, j: (0, 0)),
            pl.BlockSpec((None, d, tn), lambda i, j: (layer, 0, j)),
            pl.BlockSpec((1, tn), lambda i, j: (0, j)),
        ],
        out_specs=pl.BlockSpec((tm, tn), lambda i, j: (i, j)),
        out_shape=jax.ShapeDtypeStruct((t, n), out_dtype),
        scratch_shapes=[pltpu.VMEM((tm, d), BF16)],
        compiler_params=_params("parallel", "arbitrary"),
        name="norm_matmul",
    )(x, g, w, col_scale)


def _matmul_res_kernel(a_ref, w_ref, x_ref, o_ref):
    o_ref[...] = x_ref[...] + jnp.dot(a_ref[...], w_ref[...], preferred_element_type=F32)


def matmul_residual(a, w, layer, x, *, tm, tn):
    t, k = a.shape
    n = w.shape[2]
    return pl.pallas_call(
        _matmul_res_kernel,
        grid=(t // tm, n // tn),
        in_specs=[
            pl.BlockSpec((tm, k), lambda i, j: (i, 0)),
            pl.BlockSpec((None, k, tn), lambda i, j: (layer, 0, j)),
            pl.BlockSpec((tm, tn), lambda i, j: (i, j)),
        ],
        out_specs=pl.BlockSpec((tm, tn), lambda i, j: (i, j)),
        out_shape=jax.ShapeDtypeStruct((t, n), F32),
        compiler_params=_params("parallel", "arbitrary"),
        name="matmul_residual",
    )(a, w, x)


FFN_HALO = BF16_SUBLANES


def _ffn_kernel(x_ref, xh_ref, xr_ref, g_ref, wg_ref, wv_ref, cwg_ref, cwv_ref, cbg_ref, cbv_ref,
                wd_ref, o_ref, h_ref, z_ref, act_ref, *, nf, n_steps, tiles_per_seq, row_chunk,
                epi_rows):
    s = pl.program_id(0)
    su = jnp.minimum(s, n_steps - 1)
    sd = jnp.clip(s - 2, 0, n_steps - 1)
    tm = x_ref.shape[0]

    @pl.when(s == 0)
    def _():
        z_ref[...] = jnp.zeros(z_ref.shape, z_ref.dtype)
        act_ref[...] = jnp.zeros(act_ref.shape, act_ref.dtype)

    @pl.when(su % nf == 0)
    def _():
        keep = (((su // nf) % tiles_per_seq) != 0).astype(F32)
        h_ref[0:FFN_HALO, :] = (_rms_rows(xh_ref[...], g_ref[...]) * keep).astype(h_ref.dtype)

        def body(r, c):
            r0 = pl.multiple_of(r * row_chunk, row_chunk)
            h_ref[pl.ds(FFN_HALO + r0, row_chunk), :] = _rms_rows(
                x_ref[pl.ds(r0, row_chunk), :], g_ref[...]).astype(h_ref.dtype)
            return c
        lax.fori_loop(0, tm // row_chunk, body, 0)

    tf, d = wd_ref.shape

    def conv(zc, cw_ref, cb_ref, lanes):
        z1 = pltpu.roll(zc, 1, 0)[SUBLANES:]
        z2 = pltpu.roll(zc, 2, 0)[SUBLANES:]
        return (cw_ref[2:3, lanes] * zc[SUBLANES:] + cw_ref[1:2, lanes] * z1
                + cw_ref[0:1, lanes] * z2 + cb_ref[:, lanes])

    def stages(par):
        def up_piece(k, w_ref, n0):
            cols = slice(n0, n0 + MXU_COLS)
            z_ref[par, k, :, cols] = jnp.dot(h_ref[...], w_ref[:, cols], preferred_element_type=F32)

        def down_piece(n0):
            cols = slice(n0, n0 + MXU_COLS)
            o_ref[:, cols] += jnp.dot(act_ref[par], wd_ref[:, cols], preferred_element_type=F32)

        def epilogue_block(r0, l0):
            rows = slice(FFN_HALO - SUBLANES + r0, FFN_HALO + r0 + epi_rows)
            lanes = slice(l0, l0 + LANES)
            gate = conv(z_ref[1 - par, 0, rows, lanes], cwg_ref, cbg_ref, lanes)
            val = conv(z_ref[1 - par, 1, rows, lanes], cwv_ref, cbv_ref, lanes)
            act_ref[1 - par, r0:r0 + epi_rows, lanes] = (
                gate * jax.nn.sigmoid(gate) * val).astype(BF16)

        mxu = ([(tm * d, functools.partial(up_piece, k, w, n0))
                for k, w in ((0, wg_ref), (1, wv_ref)) for n0 in range(0, tf, MXU_COLS)]
               + [(tm * tf, functools.partial(down_piece, n0)) for n0 in range(0, d, MXU_COLS)])
        vpu = [functools.partial(epilogue_block, r0, l0)
               for r0 in range(0, tm, epi_rows) for l0 in range(0, tf, LANES)]
        total = sum(w for w, _ in mxu[:-1])
        done, issued = 0, 0
        for w, piece in mxu:
            piece()
            done += w
            while issued < len(vpu) and issued * total < done * len(vpu):
                vpu[issued]()
                issued += 1

    @pl.when(sd % nf == 0)
    def _():
        o_ref[...] = xr_ref[...]

    for par in range(2):
        pl.when(s % 2 == par)(functools.partial(stages, par))


def conv_ffn(x, g, w_up, conv_w, conv_b, w_down, layer, *, seq, tm, tf):
    t, d = x.shape
    dff = w_down.shape[1]
    nf = dff // tf
    n_steps = (t // tm) * nf
    halo_blocks = tm // FFN_HALO
    up = lambda s: jnp.minimum(s, n_steps - 1)
    ep = lambda s: jnp.clip(s - 1, 0, n_steps - 1)
    dn = lambda s: jnp.clip(s - 2, 0, n_steps - 1)
    return pl.pallas_call(
        functools.partial(_ffn_kernel, nf=nf, n_steps=n_steps, tiles_per_seq=seq // tm,
                          row_chunk=min(tm, 128), epi_rows=min(tm, 64)),
        grid=(n_steps + 2,),
        in_specs=[
            pl.BlockSpec((tm, d), lambda s: (up(s) // nf, 0)),
            pl.BlockSpec((FFN_HALO, d),
                         lambda s: (jnp.maximum((up(s) // nf) * halo_blocks - 1, 0), 0)),
            pl.BlockSpec((tm, d), lambda s: (dn(s) // nf, 0)),
            pl.BlockSpec((1, d), lambda s: (0, 0)),
            pl.BlockSpec((None, d, tf), lambda s: (layer, 0, up(s) % nf)),
            pl.BlockSpec((None, d, tf), lambda s: (layer, 0, up(s) % nf + nf)),
            pl.BlockSpec((3, tf), lambda s: (0, ep(s) % nf)),
            pl.BlockSpec((3, tf), lambda s: (0, ep(s) % nf + nf)),
            pl.BlockSpec((1, tf), lambda s: (0, ep(s) % nf)),
            pl.BlockSpec((1, tf), lambda s: (0, ep(s) % nf + nf)),
            pl.BlockSpec((None, tf, d), lambda s: (layer, dn(s) % nf, 0)),
        ],
        out_specs=pl.BlockSpec((tm, d), lambda s: (dn(s) // nf, 0)),
        out_shape=jax.ShapeDtypeStruct((t, d), F32),
        scratch_shapes=[
            pltpu.VMEM((FFN_HALO + tm, d), BF16),
            pltpu.VMEM((2, 2, FFN_HALO + tm, tf), F32),
            pltpu.VMEM((2, tm, tf), BF16),
        ],
        compiler_params=_params("arbitrary"),
        name="conv_ffn",
    )(x, x, x, g, w_up, w_up, conv_w, conv_w, conv_b, conv_b, w_down)


MIX_HALO = 32


def _even_mix_kernel(av_ref, ag_ref, avh_ref, agh_ref, u_ref, v_ref, cw_ref, cb_ref,
                     lag_ref, lab_ref, lvg_ref, lvb_ref, ws_ref, bst_ref, o_ref,
                     a_scr, y_scr, v_scr, *, tiles_per_seq, width, row_chunk):
    i = pl.program_id(0)
    ts, c = av_ref.shape
    ext = ts + MIX_HALO
    heads, chunk, _ = ws_ref.shape

    keep = ((i % tiles_per_seq) != 0).astype(F32)
    a_scr[0, 0:MIX_HALO, :] = avh_ref[...] * jax.nn.sigmoid(agh_ref[...]) * keep
    a_scr[0, MIX_HALO:ext, :] = av_ref[...] * jax.nn.sigmoid(ag_ref[...])
    for r in range(1, SUBLANES):
        a_scr[r, 0:ext - SUBLANES, :] = a_scr[0, r:r + ext - SUBLANES, :]

    base = MIX_HALO - (width - 1)

    def conv_body(rc, carry):
        r0 = pl.multiple_of(rc * row_chunk, row_chunk)
        for l0 in range(0, c, LANES):
            acc = jnp.broadcast_to(cb_ref[:, l0:l0 + LANES], (row_chunk, LANES))
            for k in range(width):
                off = base + k
                acc = acc + cw_ref[k:k + 1, l0:l0 + LANES] * a_scr[
                    off % SUBLANES, pl.ds(r0 + (off // SUBLANES) * SUBLANES, row_chunk), l0:l0 + LANES]
            y_scr[pl.ds(r0, row_chunk), l0:l0 + LANES] = acc
        return carry
    lax.fori_loop(0, ts // row_chunk, conv_body, 0)

    def ln_body(rc, carry):
        r0 = pl.multiple_of(rc * row_chunk, row_chunk)
        ya = _layernorm_rows(y_scr[pl.ds(r0, row_chunk), :], lag_ref[...], lab_ref[...])
        o_ref[pl.ds(r0, row_chunk), 0:c] = (ya * jax.nn.sigmoid(ya)).astype(o_ref.dtype)
        vv = _layernorm_rows(jax.nn.gelu(v_ref[pl.ds(r0, row_chunk), :]), lvg_ref[...], lvb_ref[...])
        v_scr[pl.ds(r0, row_chunk), :] = vv.astype(v_scr.dtype)
        return carry
    lax.fori_loop(0, ts // row_chunk, ln_body, 0)

    hd = c // heads
    row = lax.broadcasted_iota(jnp.int32, (chunk, chunk), 0)
    col = lax.broadcasted_iota(jnp.int32, (chunk, chunk), 1)
    for hh in range(heads):
        wm = jnp.where(col <= row, ws_ref[hh], 0.0).astype(BF16)
        bias = bst_ref[:, hh:hh + 1]
        for cc in range(ts // chunk):
            rows = slice(cc * chunk, (cc + 1) * chunk)
            cols = slice(hh * hd, (hh + 1) * hd)
            s = jnp.dot(wm, v_scr[rows, cols], preferred_element_type=F32) + bias
            o_ref[rows, c + hh * hd:c + (hh + 1) * hd] = (
                jax.nn.gelu(u_ref[rows, cols]) * s).astype(o_ref.dtype)


def even_mix(z, conv_w, conv_b, ln_a_g, ln_a_b, ln_v_g, ln_v_b, w_s, b_s_t, *, seq, ts):
    t = z.shape[0]
    c = z.shape[1] // 4
    width = conv_w.shape[0]
    heads, chunk, _ = w_s.shape
    assert width - 1 <= MIX_HALO and ts % chunk == 0 and seq % ts == 0
    halo_blocks = ts // MIX_HALO
    row = lambda i: (i, 0)
    vec = pl.BlockSpec((1, c), lambda i: (0, 0))
    return pl.pallas_call(
        functools.partial(_even_mix_kernel, tiles_per_seq=seq // ts, width=width, row_chunk=32),
        grid=(t // ts,),
        in_specs=[
            pl.BlockSpec((ts, c), lambda i: (i, 0)),
            pl.BlockSpec((ts, c), lambda i: (i, 1)),
            pl.BlockSpec((MIX_HALO, c), lambda i: (jnp.maximum(i * halo_blocks - 1, 0), 0)),
            pl.BlockSpec((MIX_HALO, c), lambda i: (jnp.maximum(i * halo_blocks - 1, 0), 1)),
            pl.BlockSpec((ts, c), lambda i: (i, 2)),
            pl.BlockSpec((ts, c), lambda i: (i, 3)),
            pl.BlockSpec((width, c), lambda i: (0, 0)),
            vec, vec, vec, vec, vec,
            pl.BlockSpec((heads, chunk, chunk), lambda i: (0, 0, 0)),
            pl.BlockSpec((chunk, heads), lambda i: (0, 0)),
        ],
        out_specs=pl.BlockSpec((ts, 2 * c), row),
        out_shape=jax.ShapeDtypeStruct((t, 2 * c), BF16),
        scratch_shapes=[
            pltpu.VMEM((SUBLANES, ts + MIX_HALO, c), F32),
            pltpu.VMEM((ts, c), F32),
            pltpu.VMEM((ts, c), BF16),
        ],
        compiler_params=_params("parallel"),
        name="even_mix",
    )(z, z, z, z, z, z, conv_w, conv_b, ln_a_g, ln_a_b, ln_v_g, ln_v_b, w_s, b_s_t)


def _diff_attn_kernel(q_ref, k_ref, v_ref, lq1_ref, lk1_ref, lq2_ref, lk2_ref, sg_ref, o_ref,
                      m_scr, l_scr, acc_scr, s_scr, p_scr, alpha_scr, *, lambda_init, tk, sm_rows):
    i = pl.program_id(2)
    tq, dv = q_ref.shape
    dh = dv // 2
    lane_tiles = tk // LANES

    m_scr[...] = jnp.full(m_scr.shape, NEG_INF, F32)
    l_scr[...] = jnp.zeros(l_scr.shape, F32)
    acc_scr[...] = jnp.zeros(acc_scr.shape, F32)

    def step(j, r0, diagonal):
        k0 = pl.multiple_of(j * tk, tk)
        half = (tq - r0) // 2
        units = [(c, b0) for c in range(2) for b0 in (r0, r0 + half)]

        def qk(c, b0):
            rows = slice(b0, b0 + half)
            s_scr[c, rows, :] = lax.dot_general(
                q_ref[rows, c * dh:(c + 1) * dh], k_ref[pl.ds(k0, tk), c * dh:(c + 1) * dh],
                (((1,), (1,)), ((), ())), preferred_element_type=F32)

        def softmax_group(c, g0):
            rows = slice(g0, g0 + sm_rows)
            s = s_scr[c, rows, :]
            if diagonal and g0 - r0 < tk:
                rel = lax.broadcasted_iota(jnp.int32, (sm_rows, tk), 0) + (g0 - r0)
                col = lax.broadcasted_iota(jnp.int32, (sm_rows, tk), 1)
                s = jnp.where(col <= rel, s, NEG_INF)
            tiles = [s[:, t * LANES:(t + 1) * LANES] for t in range(lane_tiles)]
            tile_max = functools.reduce(jnp.maximum, tiles)
            m_prev = m_scr[c, rows, :]
            m_new = jnp.maximum(m_prev, jnp.max(tile_max, axis=-1, keepdims=True))
            alpha = jnp.exp2(m_prev - m_new)
            p_tiles = [jnp.exp2(t - m_new) for t in tiles]
            l_scr[c, rows, :] = alpha * l_scr[c, rows, :] + functools.reduce(jnp.add, p_tiles)
            p_scr[c, rows, :] = jnp.concatenate(p_tiles, axis=1).astype(BF16)
            alpha_scr[c, rows, :] = alpha
            m_scr[c, rows, :] = m_new

        def pv(c, b0):
            rows = slice(b0, b0 + half)
            alpha_v = jnp.concatenate([alpha_scr[c, rows, :]] * (dv // LANES), axis=1)
            acc_scr[c, rows, :] = alpha_v * acc_scr[c, rows, :] + jnp.dot(
                p_scr[c, rows, :], v_ref[pl.ds(k0, tk), :], preferred_element_type=F32)

        def sm(u):
            c, b0 = units[u]
            return [functools.partial(softmax_group, c, g0) for g0 in range(b0, b0 + half, sm_rows)]

        stages = [
            ([(qk, 0)], []),
            ([(qk, 1)], sm(0)),
            ([(qk, 2)], sm(1)),
            ([(pv, 0), (qk, 3)], sm(2)),
            ([(pv, 1), (pv, 2)], sm(3)),
            ([(pv, 3)], []),
        ]
        for pieces, groups in stages:
            per_piece = -(-len(groups) // len(pieces))
            for n, (fn, u) in enumerate(pieces):
                fn(*units[u])
                for grp in groups[n * per_piece:(n + 1) * per_piece]:
                    grp()

    n_full = i * (tq // tk)

    def body(j, carry):
        step(j, 0, False)
        return carry
    lax.fori_loop(0, n_full, body, 0)
    for d in range(tq // tk):
        step(n_full + d, d * tk, True)

    lam = (jnp.exp(jnp.sum(lq1_ref[...] * lk1_ref[...], axis=-1, keepdims=True))
           - jnp.exp(jnp.sum(lq2_ref[...] * lk2_ref[...], axis=-1, keepdims=True))
           + lambda_init)
    l1 = jnp.sum(l_scr[0], axis=-1, keepdims=True)
    l2 = jnp.sum(l_scr[1], axis=-1, keepdims=True)
    o = acc_scr[0] / l1 - lam * (acc_scr[1] / l2)
    o_ref[...] = (_rms_rows(o, sg_ref[...]) * (1.0 - lambda_init)).astype(o_ref.dtype)


def diff_attention(qkv, lq1, lk1, lq2, lk2, subln_g, *, batch, seq, heads, lambda_init, tq, tk):
    t = qkv.shape[0]
    dv = subln_g.shape[1]
    dh = lq1.shape[1]
    assert dv == 2 * dh and tq % tk == 0 and tk % LANES == 0 and tk % 64 == 0
    nq = seq // tq
    vec = lambda n: pl.BlockSpec((1, n), lambda b, h, i: (0, 0))
    return pl.pallas_call(
        functools.partial(_diff_attn_kernel, lambda_init=lambda_init, tk=tk, sm_rows=32),
        grid=(batch, heads, nq),
        in_specs=[
            pl.BlockSpec((tq, dv), lambda b, h, i: (b * nq + i, h)),
            pl.BlockSpec((seq, dv), lambda b, h, i: (b, heads + h)),
            pl.BlockSpec((seq, dv), lambda b, h, i: (b, 2 * heads + h)),
            vec(dh), vec(dh), vec(dh), vec(dh), vec(dv),
        ],
        out_specs=pl.BlockSpec((tq, dv), lambda b, h, i: (b * nq + i, h)),
        out_shape=jax.ShapeDtypeStruct((t, heads * dv), BF16),
        scratch_shapes=[
            pltpu.VMEM((2, tq, LANES), F32),
            pltpu.VMEM((2, tq, LANES), F32),
            pltpu.VMEM((2, tq, dv), F32),
            pltpu.VMEM((2, tq, tk), F32),
            pltpu.VMEM((2, tq, tk), BF16),
            pltpu.VMEM((2, tq, LANES), F32),
        ],
        compiler_params=_params("parallel", "parallel", "arbitrary"),
        name="diff_attention",
    )(qkv, qkv, qkv, lq1, lk1, lq2, lk2, subln_g)


def _rmsnorm_kernel(x_ref, g_ref, o_ref):
    o_ref[...] = _rms_rows(x_ref[...], g_ref[...])


def rmsnorm(x, g, *, tm):
    t, d = x.shape
    return pl.pallas_call(
        _rmsnorm_kernel,
        grid=(t // tm,),
        in_specs=[pl.BlockSpec((tm, d), lambda i: (i, 0)), pl.BlockSpec((1, d), lambda i: (0, 0))],
        out_specs=pl.BlockSpec((tm, d), lambda i: (i, 0)),
        out_shape=jax.ShapeDtypeStruct((t, d), F32),
        compiler_params=_params("parallel"),
        name="final_rmsnorm",
    )(x, g)


def _tiles(seq, d_ff):
    return dict(
        proj_tm=min(1024, seq), proj_tn=1024,
        ffn_tm=min(512, seq), ffn_tf=512 if d_ff % 512 == 0 else 128,
        mix_ts=min(256, seq), attn_tq=min(2048, seq), attn_tk=min(512, seq), norm_tm=min(256, seq),
    )


def _forward(x, norm_mix_g, norm_ffn_g, final_norm_g, ev_w_in, ev_conv_w, ev_conv_b, ev_ln_a_g,
             ev_ln_a_b, ev_ln_v_g, ev_ln_v_b, ev_w_s, ev_b_s, ev_w_out, od_w_qkv, od_lambda_q1,
             od_lambda_k1, od_lambda_q2, od_lambda_k2, od_subln_g, od_w_o, ffn_w_up, ffn_conv_w,
             ffn_conv_b, ffn_w_down, tiles=None):
    batch, seq, d = x.shape
    depth = norm_mix_g.shape[0]
    dh = od_lambda_q1.shape[-1]
    dv = od_subln_g.shape[-1]
    heads = od_w_o.shape[1] // dv
    qk_w = heads * 2 * dh
    tl = dict(_tiles(seq, ffn_w_down.shape[1]))
    if tiles:
        tl.update(tiles)

    row = lambda v: v.reshape(1, -1)
    xt = x.reshape(batch * seq, d)
    w_in, w_out, w_qkv, w_o, w_up, w_down = (
        w.astype(BF16) for w in (ev_w_in, ev_w_out, od_w_qkv, od_w_o, ffn_w_up, ffn_w_down))
    for i in range(depth):
        j = i // 2
        if i % 2 == 0:
            ones = jnp.ones((1, w_in.shape[2]), F32)
            z = norm_matmul(xt, row(norm_mix_g[i]), w_in, j, ones, F32,
                            tm=tl["proj_tm"], tn=tl["proj_tn"])
            cat = even_mix(z, ev_conv_w[j], row(ev_conv_b[j]), row(ev_ln_a_g[j]), row(ev_ln_a_b[j]),
                           row(ev_ln_v_g[j]), row(ev_ln_v_b[j]), ev_w_s[j], ev_b_s[j].T,
                           seq=seq, ts=tl["mix_ts"])
            xt = matmul_residual(cat, w_out, j, xt, tm=tl["proj_tm"], tn=tl["proj_tn"])
        else:
            lambda_init = 0.8 - 0.6 * math.exp(-0.3 * i)
            col_scale = jnp.concatenate(
                [jnp.full((1, qk_w), math.log2(math.e) / math.sqrt(dh), F32),
                 jnp.ones((1, w_qkv.shape[2] - qk_w), F32)], axis=1)
            qkv = norm_matmul(xt, row(norm_mix_g[i]), w_qkv, j, col_scale, BF16,
                              tm=tl["proj_tm"], tn=tl["proj_tn"])
            o = diff_attention(qkv, row(od_lambda_q1[j]), row(od_lambda_k1[j]), row(od_lambda_q2[j]),
                               row(od_lambda_k2[j]), row(od_subln_g[j]), batch=batch, seq=seq,
                               heads=heads, lambda_init=lambda_init, tq=tl["attn_tq"], tk=tl["attn_tk"])
            xt = matmul_residual(o, w_o, j, xt, tm=tl["proj_tm"], tn=tl["proj_tn"])
        xt = conv_ffn(xt, row(norm_ffn_g[i]), w_up, ffn_conv_w[i], row(ffn_conv_b[i]), w_down, i,
                      seq=seq, tm=tl["ffn_tm"], tf=tl["ffn_tf"])
    out = rmsnorm(xt, row(final_norm_g), tm=tl["norm_tm"])
    return out.reshape(batch, seq, d)


def kernel(x, norm_mix_g, norm_ffn_g, final_norm_g, ev_w_in, ev_conv_w, ev_conv_b, ev_ln_a_g, ev_ln_a_b, ev_ln_v_g, ev_ln_v_b, ev_w_s, ev_b_s, ev_w_out, od_w_qkv, od_lambda_q1, od_lambda_k1, od_lambda_q2, od_lambda_k2, od_subln_g, od_w_o, ffn_w_up, ffn_conv_w, ffn_conv_b, ffn_w_down):
    return _forward(x, norm_mix_g, norm_ffn_g, final_norm_g, ev_w_in, ev_conv_w, ev_conv_b, ev_ln_a_g,
                    ev_ln_a_b, ev_ln_v_g, ev_ln_v_b, ev_w_s, ev_b_s, ev_w_out, od_w_qkv, od_lambda_q1,
                    od_lambda_k1, od_lambda_q2, od_lambda_k2, od_subln_g, od_w_o, ffn_w_up,
                    ffn_conv_w, ffn_conv_b, ffn_w_down)
```

```python
import functools
import math

import jax
import jax.numpy as jnp
from jax import lax
from jax.experimental import pallas as pl
from jax.experimental.pallas import tpu as pltpu

EPS = 1e-6
NEG_INF = -1e30
V7X_VMEM_BYTES = 64 * 1024 * 1024
VMEM_LIMIT_BYTES = V7X_VMEM_BYTES - 8 * 1024 * 1024
SUBLANES = 8
LANES = 128
BF16_SUBLANES = 16
MXU_COLS = 256

F32 = jnp.float32
BF16 = jnp.bfloat16


def _params(*sem):
    return pltpu.CompilerParams(dimension_semantics=sem, vmem_limit_bytes=VMEM_LIMIT_BYTES)


def _rms_rows(x, g):
    ms = jnp.mean(x * x, axis=-1, keepdims=True)
    return x * lax.rsqrt(ms + EPS) * g


def _layernorm_rows(x, g, b):
    mu = jnp.mean(x, axis=-1, keepdims=True)
    xc = x - mu
    var = jnp.mean(xc * xc, axis=-1, keepdims=True)
    return xc * lax.rsqrt(var + EPS) * g + b


def _norm_matmul_kernel(x_ref, g_ref, w_ref, cs_ref, o_ref, h_ref, *, row_chunk, tn):
    def body(r, c):
        r0 = pl.multiple_of(r * row_chunk, row_chunk)
        h_ref[pl.ds(r0, row_chunk), :] = _rms_rows(
            x_ref[pl.ds(r0, row_chunk), :], g_ref[...]).astype(h_ref.dtype)
        return c
    lax.fori_loop(0, x_ref.shape[0] // row_chunk, body, 0)

    for n0 in range(0, o_ref.shape[1], tn):
        cols = slice(n0, n0 + tn)
        acc = jnp.dot(h_ref[...], w_ref[:, cols], preferred_element_type=F32)
        o_ref[:, cols] = (acc * cs_ref[:, cols]).astype(o_ref.dtype)


def norm_matmul(x, g, w, layer, col_scale, out_dtype, *, tm, tn):
    t, d = x.shape
    n = w.shape[2]
    return pl.pallas_call(
        functools.partial(_norm_matmul_kernel, row_chunk=min(tm, 128), tn=tn),
        grid=(t // tm,),
        in_specs=[
            pl.BlockSpec((tm, d), lambda i: (i, 0)),
            pl.BlockSpec((1, d), lambda i: (0, 0)),
            pl.BlockSpec((None, d, n), lambda i: (layer, 0, 0), pipeline_mode=pl.Buffered(1)),
            pl.BlockSpec((1, n), lambda i: (0, 0)),
        ],
        out_specs=pl.BlockSpec((tm, n), lambda i: (i, 0)),
        out_shape=jax.ShapeDtypeStruct((t, n), out_dtype),
        scratch_shapes=[pltpu.VMEM((tm, d), BF16)],
        compiler_params=_params("parallel"),
        name="norm_matmul",
    )(x, g, w, col_scale)


def _matmul_res_kernel(a_ref, w_ref, x_ref, o_ref, *, tn):
    for n0 in range(0, o_ref.shape[1], tn):
        cols = slice(n0, n0 + tn)
        o_ref[:, cols] = x_ref[:, cols] + jnp.dot(a_ref[...], w_ref[:, cols],
                                                  preferred_element_type=F32)


def matmul_residual(a, w, layer, x, *, tm, tn):
    t, k = a.shape
    n = w.shape[2]
    return pl.pallas_call(
        functools.partial(_matmul_res_kernel, tn=tn),
        grid=(t // tm,),
        in_specs=[
            pl.BlockSpec((tm, k), lambda i: (i, 0)),
            pl.BlockSpec((None, k, n), lambda i: (layer, 0, 0), pipeline_mode=pl.Buffered(1)),
            pl.BlockSpec((tm, n), lambda i: (i, 0)),
        ],
        out_specs=pl.BlockSpec((tm, n), lambda i: (i, 0)),
        out_shape=jax.ShapeDtypeStruct((t, n), F32),
        compiler_params=_params("parallel"),
        name="matmul_residual",
    )(a, w, x)


FFN_HALO = BF16_SUBLANES


def _ffn_kernel(x_ref, xh_ref, g_ref, wg_ref, wv_ref, cwg_ref, cwv_ref, cbg_ref, cbv_ref,
                wd_ref, o_ref, h_ref, z_ref, act_ref, *, nf, n_steps, tiles_per_seq, row_chunk,
                epi_rows):
    s = pl.program_id(0)
    su = jnp.minimum(s, n_steps - 1)
    sd = jnp.clip(s - 2, 0, n_steps - 1)
    tm = x_ref.shape[0]

    @pl.when(s == 0)
    def _():
        z_ref[...] = jnp.zeros(z_ref.shape, z_ref.dtype)
        act_ref[...] = jnp.zeros(act_ref.shape, act_ref.dtype)

    @pl.when(su % nf == 0)
    def _():
        keep = (((su // nf) % tiles_per_seq) != 0).astype(F32)
        h_ref[0:FFN_HALO, :] = (_rms_rows(xh_ref[...], g_ref[...]) * keep).astype(h_ref.dtype)

        def body(r, c):
            r0 = pl.multiple_of(r * row_chunk, row_chunk)
            h_ref[pl.ds(FFN_HALO + r0, row_chunk), :] = _rms_rows(
                x_ref[pl.ds(r0, row_chunk), :], g_ref[...]).astype(h_ref.dtype)
            return c
        lax.fori_loop(0, tm // row_chunk, body, 0)

    tf, d = wd_ref.shape

    def conv(zc, cw_ref, cb_ref, lanes):
        z1 = pltpu.roll(zc, 1, 0)[SUBLANES:]
        z2 = pltpu.roll(zc, 2, 0)[SUBLANES:]
        return (cw_ref[2:3, lanes] * zc[SUBLANES:] + cw_ref[1:2, lanes] * z1
                + cw_ref[0:1, lanes] * z2 + cb_ref[:, lanes])

    def stages(par):
        def up_piece(k, w_ref, n0):
            cols = slice(n0, n0 + MXU_COLS)
            z_ref[par, k, :, cols] = jnp.dot(h_ref[...], w_ref[:, cols], preferred_element_type=F32)

        def down_piece(n0):
            cols = slice(n0, n0 + MXU_COLS)
            o_ref[:, cols] += jnp.dot(act_ref[par], wd_ref[:, cols], preferred_element_type=F32)

        def epilogue_block(r0, l0):
            rows = slice(FFN_HALO - SUBLANES + r0, FFN_HALO + r0 + epi_rows)
            lanes = slice(l0, l0 + LANES)
            gate = conv(z_ref[1 - par, 0, rows, lanes], cwg_ref, cbg_ref, lanes)
            val = conv(z_ref[1 - par, 1, rows, lanes], cwv_ref, cbv_ref, lanes)
            act_ref[1 - par, r0:r0 + epi_rows, lanes] = (
                gate * jax.nn.sigmoid(gate) * val).astype(BF16)

        mxu = ([(tm * d, functools.partial(up_piece, k, w, n0))
                for k, w in ((0, wg_ref), (1, wv_ref)) for n0 in range(0, tf, MXU_COLS)]
               + [(tm * tf, functools.partial(down_piece, n0)) for n0 in range(0, d, MXU_COLS)])
        vpu = [functools.partial(epilogue_block, r0, l0)
               for r0 in range(0, tm, epi_rows) for l0 in range(0, tf, LANES)]
        total = sum(w for w, _ in mxu[:-1])
        done, issued = 0, 0
        for w, piece in mxu:
            piece()
            done += w
            while issued < len(vpu) and issued * total < done * len(vpu):
                vpu[issued]()
                issued += 1

    @pl.when(sd % nf == 0)
    def _():
        o_ref[...] = x_ref[...]

    for par in range(2):
        pl.when(s % 2 == par)(functools.partial(stages, par))


def conv_ffn(x, g, w_up, conv_w, conv_b, w_down, layer, *, seq, tm, tf):
    t, d = x.shape
    dff = w_down.shape[1]
    nf = dff // tf
    assert nf >= 3 and w_up.shape[1:] == (2 * nf, d, tf)
    n_steps = (t // tm) * nf
    halo_blocks = tm // FFN_HALO
    up = lambda s: jnp.minimum(s, n_steps - 1)
    ep = lambda s: jnp.clip(s - 1, 0, n_steps - 1)
    dn = lambda s: jnp.clip(s - 2, 0, n_steps - 1)
    return pl.pallas_call(
        functools.partial(_ffn_kernel, nf=nf, n_steps=n_steps, tiles_per_seq=seq // tm,
                          row_chunk=min(tm, 128), epi_rows=min(tm, 64)),
        grid=(n_steps + 2,),
        in_specs=[
            pl.BlockSpec((tm, d), lambda s: (up(s) // nf, 0)),
            pl.BlockSpec((FFN_HALO, d),
                         lambda s: (jnp.maximum((up(s) // nf) * halo_blocks - 1, 0), 0)),
            pl.BlockSpec((1, d), lambda s: (0, 0)),
            pl.BlockSpec((None, None, d, tf), lambda s: (layer, up(s) % nf, 0, 0)),
            pl.BlockSpec((None, None, d, tf), lambda s: (layer, up(s) % nf + nf, 0, 0)),
            pl.BlockSpec((3, tf), lambda s: (0, ep(s) % nf)),
            pl.BlockSpec((3, tf), lambda s: (0, ep(s) % nf + nf)),
            pl.BlockSpec((1, tf), lambda s: (0, ep(s) % nf)),
            pl.BlockSpec((1, tf), lambda s: (0, ep(s) % nf + nf)),
            pl.BlockSpec((None, tf, d), lambda s: (layer, dn(s) % nf, 0)),
        ],
        out_specs=pl.BlockSpec((tm, d), lambda s: (dn(s) // nf, 0)),
        out_shape=jax.ShapeDtypeStruct((t, d), F32),
        scratch_shapes=[
            pltpu.VMEM((FFN_HALO + tm, d), BF16),
            pltpu.VMEM((2, 2, FFN_HALO + tm, tf), F32),
            pltpu.VMEM((2, tm, tf), BF16),
        ],
        compiler_params=_params("arbitrary"),
        name="conv_ffn",
    )(x, x, g, w_up, w_up, conv_w, conv_w, conv_b, conv_b, w_down)


MIX_HALO = 32


def _even_mix_kernel(av_ref, ag_ref, avh_ref, agh_ref, u_ref, v_ref, cw_ref, cb_ref,
                     lag_ref, lab_ref, lvg_ref, lvb_ref, ws_ref, bst_ref, o_ref,
                     a_scr, y_scr, v_scr, *, tiles_per_seq, width, row_chunk):
    i = pl.program_id(0)
    ts, c = av_ref.shape
    ext = ts + MIX_HALO
    heads, chunk, _ = ws_ref.shape

    keep = ((i % tiles_per_seq) != 0).astype(F32)
    a_scr[0, 0:MIX_HALO, :] = avh_ref[...] * jax.nn.sigmoid(agh_ref[...]) * keep
    a_scr[0, MIX_HALO:ext, :] = av_ref[...] * jax.nn.sigmoid(ag_ref[...])
    for r in range(1, SUBLANES):
        a_scr[r, 0:ext - SUBLANES, :] = a_scr[0, r:r + ext - SUBLANES, :]

    base = MIX_HALO - (width - 1)

    def conv_body(rc, carry):
        r0 = pl.multiple_of(rc * row_chunk, row_chunk)
        for l0 in range(0, c, LANES):
            acc = jnp.broadcast_to(cb_ref[:, l0:l0 + LANES], (row_chunk, LANES))
            for k in range(width):
                off = base + k
                acc = acc + cw_ref[k:k + 1, l0:l0 + LANES] * a_scr[
                    off % SUBLANES, pl.ds(r0 + (off // SUBLANES) * SUBLANES, row_chunk), l0:l0 + LANES]
            y_scr[pl.ds(r0, row_chunk), l0:l0 + LANES] = acc
        return carry
    lax.fori_loop(0, ts // row_chunk, conv_body, 0)

    def ln_body(rc, carry):
        r0 = pl.multiple_of(rc * row_chunk, row_chunk)
        ya = _layernorm_rows(y_scr[pl.ds(r0, row_chunk), :], lag_ref[...], lab_ref[...])
        o_ref[pl.ds(r0, row_chunk), 0:c] = (ya * jax.nn.sigmoid(ya)).astype(o_ref.dtype)
        vv = _layernorm_rows(jax.nn.gelu(v_ref[pl.ds(r0, row_chunk), :]), lvg_ref[...], lvb_ref[...])
        v_scr[pl.ds(r0, row_chunk), :] = vv.astype(v_scr.dtype)
        return carry
    lax.fori_loop(0, ts // row_chunk, ln_body, 0)

    hd = c // heads
    row = lax.broadcasted_iota(jnp.int32, (chunk, chunk), 0)
    col = lax.broadcasted_iota(jnp.int32, (chunk, chunk), 1)
    for hh in range(heads):
        wm = jnp.where(col <= row, ws_ref[hh], 0.0).astype(BF16)
        bias = bst_ref[:, hh:hh + 1]
        for cc in range(ts // chunk):
            rows = slice(cc * chunk, (cc + 1) * chunk)
            cols = slice(hh * hd, (hh + 1) * hd)
            s = jnp.dot(wm, v_scr[rows, cols], preferred_element_type=F32) + bias
            o_ref[rows, c + hh * hd:c + (hh + 1) * hd] = (
                jax.nn.gelu(u_ref[rows, cols]) * s).astype(o_ref.dtype)


def even_mix(z, conv_w, conv_b, ln_a_g, ln_a_b, ln_v_g, ln_v_b, w_s, b_s_t, *, seq, ts):
    t = z.shape[0]
    c = z.shape[1] // 4
    width = conv_w.shape[0]
    heads, chunk, _ = w_s.shape
    assert width - 1 <= MIX_HALO and ts % chunk == 0 and seq % ts == 0
    halo_blocks = ts // MIX_HALO
    row = lambda i: (i, 0)
    vec = pl.BlockSpec((1, c), lambda i: (0, 0))
    return pl.pallas_call(
        functools.partial(_even_mix_kernel, tiles_per_seq=seq // ts, width=width, row_chunk=32),
        grid=(t // ts,),
        in_specs=[
            pl.BlockSpec((ts, c), lambda i: (i, 0)),
            pl.BlockSpec((ts, c), lambda i: (i, 1)),
            pl.BlockSpec((MIX_HALO, c), lambda i: (jnp.maximum(i * halo_blocks - 1, 0), 0)),
            pl.BlockSpec((MIX_HALO, c), lambda i: (jnp.maximum(i * halo_blocks - 1, 0), 1)),
            pl.BlockSpec((ts, c), lambda i: (i, 2)),
            pl.BlockSpec((ts, c), lambda i: (i, 3)),
            pl.BlockSpec((width, c), lambda i: (0, 0)),
            vec, vec, vec, vec, vec,
            pl.BlockSpec((heads, chunk, chunk), lambda i: (0, 0, 0)),
            pl.BlockSpec((chunk, heads), lambda i: (0, 0)),
        ],
        out_specs=pl.BlockSpec((ts, 2 * c), row),
        out_shape=jax.ShapeDtypeStruct((t, 2 * c), BF16),
        scratch_shapes=[
            pltpu.VMEM((SUBLANES, ts + MIX_HALO, c), F32),
            pltpu.VMEM((ts, c), F32),
            pltpu.VMEM((ts, c), BF16),
        ],
        compiler_params=_params("parallel"),
        name="even_mix",
    )(z, z, z, z, z, z, conv_w, conv_b, ln_a_g, ln_a_b, ln_v_g, ln_v_b, w_s, b_s_t)


def _diff_attn_kernel(q_ref, k_ref, v_ref, lq1_ref, lk1_ref, lq2_ref, lk2_ref, sg_ref, o_ref,
                      m_scr, l_scr, acc_scr, s_scr, p_scr, alpha_scr, *, lambda_init, tk, sm_rows):
    i = pl.program_id(2)
    tq, dv = q_ref.shape
    dh = dv // 2
    lane_tiles = tk // LANES

    m_scr[...] = jnp.full(m_scr.shape, NEG_INF, F32)
    l_scr[...] = jnp.zeros(l_scr.shape, F32)
    acc_scr[...] = jnp.zeros(acc_scr.shape, F32)

    def step(j, r0, diagonal):
        k0 = pl.multiple_of(j * tk, tk)
        half = (tq - r0) // 2
        units = [(c, b0) for c in range(2) for b0 in (r0, r0 + half)]

        def qk(c, b0):
            rows = slice(b0, b0 + half)
            s_scr[c, rows, :] = lax.dot_general(
                q_ref[rows, c * dh:(c + 1) * dh], k_ref[pl.ds(k0, tk), c * dh:(c + 1) * dh],
                (((1,), (1,)), ((), ())), preferred_element_type=F32)

        def softmax_group(c, g0):
            rows = slice(g0, g0 + sm_rows)
            s = s_scr[c, rows, :]
            if diagonal and g0 - r0 < tk:
                rel = lax.broadcasted_iota(jnp.int32, (sm_rows, tk), 0) + (g0 - r0)
                col = lax.broadcasted_iota(jnp.int32, (sm_rows, tk), 1)
                s = jnp.where(col <= rel, s, NEG_INF)
            tiles = [s[:, t * LANES:(t + 1) * LANES] for t in range(lane_tiles)]
            tile_max = functools.reduce(jnp.maximum, tiles)
            m_prev = m_scr[c, rows, :]
            m_new = jnp.maximum(m_prev, jnp.max(tile_max, axis=-1, keepdims=True))
            alpha = jnp.exp2(m_prev - m_new)
            p_tiles = [jnp.exp2(t - m_new) for t in tiles]
            l_scr[c, rows, :] = alpha * l_scr[c, rows, :] + functools.reduce(jnp.add, p_tiles)
            p_scr[c, rows, :] = jnp.concatenate(p_tiles, axis=1).astype(BF16)
            alpha_scr[c, rows, :] = alpha
            m_scr[c, rows, :] = m_new

        def pv(c, b0):
            rows = slice(b0, b0 + half)
            alpha_v = jnp.concatenate([alpha_scr[c, rows, :]] * (dv // LANES), axis=1)
            acc_scr[c, rows, :] = alpha_v * acc_scr[c, rows, :] + jnp.dot(
                p_scr[c, rows, :], v_ref[pl.ds(k0, tk), :], preferred_element_type=F32)

        def sm(u):
            c, b0 = units[u]
            return [functools.partial(softmax_group, c, g0) for g0 in range(b0, b0 + half, sm_rows)]

        stages = [
            ([(qk, 0)], []),
            ([(qk, 1)], sm(0)),
            ([(qk, 2)], sm(1)),
            ([(pv, 0), (qk, 3)], sm(2)),
            ([(pv, 1), (pv, 2)], sm(3)),
            ([(pv, 3)], []),
        ]
        for pieces, groups in stages:
            per_piece = -(-len(groups) // len(pieces))
            for n, (fn, u) in enumerate(pieces):
                fn(*units[u])
                for grp in groups[n * per_piece:(n + 1) * per_piece]:
                    grp()

    n_full = i * (tq // tk)

    def body(j, carry):
        step(j, 0, False)
        return carry
    lax.fori_loop(0, n_full, body, 0)
    for d in range(tq // tk):
        step(n_full + d, d * tk, True)

    lam = (jnp.exp(jnp.sum(lq1_ref[...] * lk1_ref[...], axis=-1, keepdims=True))
           - jnp.exp(jnp.sum(lq2_ref[...] * lk2_ref[...], axis=-1, keepdims=True))
           + lambda_init)
    l1 = jnp.sum(l_scr[0], axis=-1, keepdims=True)
    l2 = jnp.sum(l_scr[1], axis=-1, keepdims=True)
    o = acc_scr[0] / l1 - lam * (acc_scr[1] / l2)
    o_ref[...] = (_rms_rows(o, sg_ref[...]) * (1.0 - lambda_init)).astype(o_ref.dtype)


def diff_attention(qkv, lq1, lk1, lq2, lk2, subln_g, *, batch, seq, heads, lambda_init, tq, tk):
    t = qkv.shape[0]
    dv = subln_g.shape[1]
    dh = lq1.shape[1]
    assert dv == 2 * dh and tq % tk == 0 and tk % LANES == 0 and tk % 64 == 0
    nq = seq // tq
    vec = lambda n: pl.BlockSpec((1, n), lambda b, h, i: (0, 0))
    return pl.pallas_call(
        functools.partial(_diff_attn_kernel, lambda_init=lambda_init, tk=tk, sm_rows=32),
        grid=(batch, heads, nq),
        in_specs=[
            pl.BlockSpec((tq, dv), lambda b, h, i: (b * nq + i, h)),
            pl.BlockSpec((seq, dv), lambda b, h, i: (b, heads + h)),
            pl.BlockSpec((seq, dv), lambda b, h, i: (b, 2 * heads + h)),
            vec(dh), vec(dh), vec(dh), vec(dh), vec(dv),
        ],
        out_specs=pl.BlockSpec((tq, dv), lambda b, h, i: (b * nq + i, h)),
        out_shape=jax.ShapeDtypeStruct((t, heads * dv), BF16),
        scratch_shapes=[
            pltpu.VMEM((2, tq, LANES), F32),
            pltpu.VMEM((2, tq, LANES), F32),
            pltpu.VMEM((2, tq, dv), F32),
            pltpu.VMEM((2, tq, tk), F32),
            pltpu.VMEM((2, tq, tk), BF16),
            pltpu.VMEM((2, tq, LANES), F32),
        ],
        compiler_params=_params("parallel", "parallel", "arbitrary"),
        name="diff_attention",
    )(qkv, qkv, qkv, lq1, lk1, lq2, lk2, subln_g)


def _rmsnorm_kernel(x_ref, g_ref, o_ref):
    o_ref[...] = _rms_rows(x_ref[...], g_ref[...])


def rmsnorm(x, g, *, tm):
    t, d = x.shape
    return pl.pallas_call(
        _rmsnorm_kernel,
        grid=(t // tm,),
        in_specs=[pl.BlockSpec((tm, d), lambda i: (i, 0)), pl.BlockSpec((1, d), lambda i: (0, 0))],
        out_specs=pl.BlockSpec((tm, d), lambda i: (i, 0)),
        out_shape=jax.ShapeDtypeStruct((t, d), F32),
        compiler_params=_params("parallel"),
        name="final_rmsnorm",
    )(x, g)


def _tiles(seq, d_ff):
    return dict(
        proj_tm=min(512, seq), proj_tn=1024,
        ffn_tm=min(1024, seq), ffn_tf=256,
        mix_ts=min(256, seq), attn_tq=min(2048, seq), attn_tk=min(512, seq), norm_tm=min(256, seq),
    )


def _forward(x, norm_mix_g, norm_ffn_g, final_norm_g, ev_w_in, ev_conv_w, ev_conv_b, ev_ln_a_g,
             ev_ln_a_b, ev_ln_v_g, ev_ln_v_b, ev_w_s, ev_b_s, ev_w_out, od_w_qkv, od_lambda_q1,
             od_lambda_k1, od_lambda_q2, od_lambda_k2, od_subln_g, od_w_o, ffn_w_up, ffn_conv_w,
             ffn_conv_b, ffn_w_down, tiles=None):
    batch, seq, d = x.shape
    depth = norm_mix_g.shape[0]
    dh = od_lambda_q1.shape[-1]
    dv = od_subln_g.shape[-1]
    heads = od_w_o.shape[1] // dv
    qk_w = heads * 2 * dh
    tl = dict(_tiles(seq, ffn_w_down.shape[1]))
    if tiles:
        tl.update(tiles)

    row = lambda v: v.reshape(1, -1)
    xt = x.reshape(batch * seq, d)
    w_in, w_out, w_qkv, w_o, w_up, w_down = (
        w.astype(BF16) for w in (ev_w_in, ev_w_out, od_w_qkv, od_w_o, ffn_w_up, ffn_w_down))
    tf = tl["ffn_tf"]
    w_up = w_up.reshape(depth, d, w_up.shape[2] // tf, tf).transpose(0, 2, 1, 3)
    for i in range(depth):
        j = i // 2
        if i % 2 == 0:
            ones = jnp.ones((1, w_in.shape[2]), F32)
            z = norm_matmul(xt, row(norm_mix_g[i]), w_in, j, ones, F32,
                            tm=tl["proj_tm"], tn=tl["proj_tn"])
            cat = even_mix(z, ev_conv_w[j], row(ev_conv_b[j]), row(ev_ln_a_g[j]), row(ev_ln_a_b[j]),
                           row(ev_ln_v_g[j]), row(ev_ln_v_b[j]), ev_w_s[j], ev_b_s[j].T,
                           seq=seq, ts=tl["mix_ts"])
            xt = matmul_residual(cat, w_out, j, xt, tm=tl["proj_tm"], tn=tl["proj_tn"])
        else:
            lambda_init = 0.8 - 0.6 * math.exp(-0.3 * i)
            col_scale = jnp.concatenate(
                [jnp.full((1, qk_w), math.log2(math.e) / math.sqrt(dh), F32),
                 jnp.ones((1, w_qkv.shape[2] - qk_w), F32)], axis=1)
            qkv = norm_matmul(xt, row(norm_mix_g[i]), w_qkv, j, col_scale, BF16,
                              tm=tl["proj_tm"], tn=tl["proj_tn"])
            o = diff_attention(qkv, row(od_lambda_q1[j]), row(od_lambda_k1[j]), row(od_lambda_q2[j]),
                               row(od_lambda_k2[j]), row(od_subln_g[j]), batch=batch, seq=seq,
                               heads=heads, lambda_init=lambda_init, tq=tl["attn_tq"], tk=tl["attn_tk"])
            xt = matmul_residual(o, w_o, j, xt, tm=tl["proj_tm"], tn=tl["proj_tn"])
        xt = conv_ffn(xt, row(norm_ffn_g[i]), w_up, ffn_conv_w[i], row(ffn_conv_b[i]), w_down, i,
                      seq=seq, tm=tl["ffn_tm"], tf=tl["ffn_tf"])
    out = rmsnorm(xt, row(final_norm_g), tm=tl["norm_tm"])
    return out.reshape(batch, seq, d)


def kernel(x, norm_mix_g, norm_ffn_g, final_norm_g, ev_w_in, ev_conv_w, ev_conv_b, ev_ln_a_g, ev_ln_a_b, ev_ln_v_g, ev_ln_v_b, ev_w_s, ev_b_s, ev_w_out, od_w_qkv, od_lambda_q1, od_lambda_k1, od_lambda_q2, od_lambda_k2, od_subln_g, od_w_o, ffn_w_up, ffn_conv_w, ffn_conv_b, ffn_w_down):
    return _forward(x, norm_mix_g, norm_ffn_g, final_norm_g, ev_w_in, ev_conv_w, ev_conv_b, ev_ln_a_g,
                    ev_ln_a_b, ev_ln_v_g, ev_ln_v_b, ev_w_s, ev_b_s, ev_w_out, od_w_qkv, od_lambda_q1,
                    od_lambda_k1, od_lambda_q2, od_lambda_k2, od_subln_g, od_w_o, ffn_w_up,
                    ffn_conv_w, ffn_conv_b, ffn_w_down)
```

```python
import functools
import math

import jax
import jax.numpy as jnp
from jax import lax
from jax.experimental import pallas as pl
from jax.experimental.pallas import tpu as pltpu

EPS = 1e-6
NEG_INF = -1e30
V7X_VMEM_BYTES = 64 * 1024 * 1024
VMEM_LIMIT_BYTES = V7X_VMEM_BYTES - 8 * 1024 * 1024
SUBLANES = 8
LANES = 128
BF16_SUBLANES = 16

F32 = jnp.float32
BF16 = jnp.bfloat16


def _params(*sem):
    return pltpu.CompilerParams(dimension_semantics=sem, vmem_limit_bytes=VMEM_LIMIT_BYTES)


def _rms_rows(x, g):
    ms = jnp.mean(x * x, axis=-1, keepdims=True)
    return x * lax.rsqrt(ms + EPS) * g


def _layernorm_rows(x, g, b):
    mu = jnp.mean(x, axis=-1, keepdims=True)
    xc = x - mu
    var = jnp.mean(xc * xc, axis=-1, keepdims=True)
    return xc * lax.rsqrt(var + EPS) * g + b


def _norm_matmul_kernel(x_ref, g_ref, w_ref, cs_ref, o_ref, h_ref, *, row_chunk, tn):
    def body(r, c):
        r0 = pl.multiple_of(r * row_chunk, row_chunk)
        h_ref[pl.ds(r0, row_chunk), :] = _rms_rows(
            x_ref[pl.ds(r0, row_chunk), :], g_ref[...]).astype(h_ref.dtype)
        return c
    lax.fori_loop(0, x_ref.shape[0] // row_chunk, body, 0)

    for n0 in range(0, o_ref.shape[1], tn):
        cols = slice(n0, n0 + tn)
        acc = jnp.dot(h_ref[...], w_ref[:, cols], preferred_element_type=F32)
        o_ref[:, cols] = (acc * cs_ref[:, cols]).astype(o_ref.dtype)


def norm_matmul(x, g, w, layer, col_scale, out_dtype, *, tm, tn):
    t, d = x.shape
    n = w.shape[2]
    return pl.pallas_call(
        functools.partial(_norm_matmul_kernel, row_chunk=min(tm, 128), tn=tn),
        grid=(t // tm,),
        in_specs=[
            pl.BlockSpec((tm, d), lambda i: (i, 0)),
            pl.BlockSpec((1, d), lambda i: (0, 0)),
            pl.BlockSpec((None, d, n), lambda i: (layer, 0, 0), pipeline_mode=pl.Buffered(1)),
            pl.BlockSpec((1, n), lambda i: (0, 0)),
        ],
        out_specs=pl.BlockSpec((tm, n), lambda i: (i, 0)),
        out_shape=jax.ShapeDtypeStruct((t, n), out_dtype),
        scratch_shapes=[pltpu.VMEM((tm, d), BF16)],
        compiler_params=_params("parallel"),
        name="norm_matmul",
    )(x, g, w, col_scale)


def _matmul_res_kernel(a_ref, w_ref, x_ref, o_ref, *, tn):
    for n0 in range(0, o_ref.shape[1], tn):
        cols = slice(n0, n0 + tn)
        o_ref[:, cols] = x_ref[:, cols] + jnp.dot(a_ref[...], w_ref[:, cols],
                                                  preferred_element_type=F32)


def matmul_residual(a, w, layer, x, *, tm, tn):
    t, k = a.shape
    n = w.shape[2]
    return pl.pallas_call(
        functools.partial(_matmul_res_kernel, tn=tn),
        grid=(t // tm,),
        in_specs=[
            pl.BlockSpec((tm, k), lambda i: (i, 0)),
            pl.BlockSpec((None, k, n), lambda i: (layer, 0, 0), pipeline_mode=pl.Buffered(1)),
            pl.BlockSpec((tm, n), lambda i: (i, 0)),
        ],
        out_specs=pl.BlockSpec((tm, n), lambda i: (i, 0)),
        out_shape=jax.ShapeDtypeStruct((t, n), F32),
        compiler_params=_params("parallel"),
        name="matmul_residual",
    )(a, w, x)


FFN_HALO = BF16_SUBLANES


def _ffn_kernel(x_ref, xh_ref, g_ref, wg_ref, wv_ref, cwg_ref, cwv_ref, cbg_ref, cbv_ref,
                wd_ref, og_ref, o_ref, h_ref, *, tiles_per_seq, row_chunk, out_norm):
    i = pl.program_id(0)
    f = pl.program_id(1)
    tm = x_ref.shape[0]

    @pl.when(f == 0)
    def _():
        keep = ((i % tiles_per_seq) != 0).astype(F32)
        h_ref[0:FFN_HALO, :] = (_rms_rows(xh_ref[...], g_ref[...]) * keep).astype(h_ref.dtype)

        def body(r, c):
            r0 = pl.multiple_of(r * row_chunk, row_chunk)
            xr = x_ref[pl.ds(r0, row_chunk), :]
            h_ref[pl.ds(FFN_HALO + r0, row_chunk), :] = _rms_rows(xr, g_ref[...]).astype(h_ref.dtype)
            o_ref[pl.ds(r0, row_chunk), :] = xr
            return c
        lax.fori_loop(0, tm // row_chunk, body, 0)

    h = h_ref[...]

    def conv(z, cw_ref, cb_ref):
        z1 = pltpu.roll(z, 1, 0)
        z2 = pltpu.roll(z, 2, 0)
        y = cw_ref[2:3, :] * z + cw_ref[1:2, :] * z1 + cw_ref[0:1, :] * z2 + cb_ref[...]
        return y[FFN_HALO:, :]

    gate = conv(jnp.dot(h, wg_ref[...], preferred_element_type=F32), cwg_ref, cbg_ref)
    val = conv(jnp.dot(h, wv_ref[...], preferred_element_type=F32), cwv_ref, cbv_ref)
    act = (gate * jax.nn.sigmoid(gate) * val).astype(BF16)
    o_ref[...] += jnp.dot(act, wd_ref[...], preferred_element_type=F32)

    if out_norm:
        @pl.when(f == pl.num_programs(1) - 1)
        def _():
            def body(r, c):
                r0 = pl.multiple_of(r * row_chunk, row_chunk)
                o_ref[pl.ds(r0, row_chunk), :] = _rms_rows(o_ref[pl.ds(r0, row_chunk), :], og_ref[...])
                return c
            lax.fori_loop(0, tm // row_chunk, body, 0)


def conv_ffn(x, g, w_up, conv_w, conv_b, w_down, layer, out_g, *, seq, tm, tf, out_norm):
    t, d = x.shape
    dff = w_down.shape[1]
    nf = dff // tf
    halo_blocks = tm // FFN_HALO
    return pl.pallas_call(
        functools.partial(_ffn_kernel, tiles_per_seq=seq // tm, row_chunk=min(tm, 128),
                          out_norm=out_norm),
        grid=(t // tm, nf),
        in_specs=[
            pl.BlockSpec((tm, d), lambda i, f: (i, 0)),
            pl.BlockSpec((FFN_HALO, d), lambda i, f: (jnp.maximum(i * halo_blocks - 1, 0), 0)),
            pl.BlockSpec((1, d), lambda i, f: (0, 0)),
            pl.BlockSpec((None, d, tf), lambda i, f: (layer, 0, f)),
            pl.BlockSpec((None, d, tf), lambda i, f: (layer, 0, f + nf)),
            pl.BlockSpec((3, tf), lambda i, f: (0, f)),
            pl.BlockSpec((3, tf), lambda i, f: (0, f + nf)),
            pl.BlockSpec((1, tf), lambda i, f: (0, f)),
            pl.BlockSpec((1, tf), lambda i, f: (0, f + nf)),
            pl.BlockSpec((None, tf, d), lambda i, f: (layer, f, 0)),
            pl.BlockSpec((1, d), lambda i, f: (0, 0)),
        ],
        out_specs=pl.BlockSpec((tm, d), lambda i, f: (i, 0)),
        out_shape=jax.ShapeDtypeStruct((t, d), F32),
        scratch_shapes=[pltpu.VMEM((FFN_HALO + tm, d), BF16)],
        compiler_params=_params("parallel", "arbitrary"),
        name="conv_ffn",
    )(x, x, g, w_up, w_up, conv_w, conv_w, conv_b, conv_b, w_down, out_g)


MIX_HALO = 32


def _even_mix_kernel(av_ref, ag_ref, avh_ref, agh_ref, u_ref, v_ref, cw_ref, cb_ref,
                     lag_ref, lab_ref, lvg_ref, lvb_ref, ws_ref, bst_ref, o_ref,
                     a_scr, y_scr, v_scr, *, tiles_per_seq, width, row_chunk, conv_rows):
    i = pl.program_id(0)
    ts, c = av_ref.shape
    ext = ts + MIX_HALO
    heads, chunk, _ = ws_ref.shape

    keep = ((i % tiles_per_seq) != 0).astype(F32)
    a_scr[0, 0:MIX_HALO, :] = avh_ref[...] * jax.nn.sigmoid(agh_ref[...]) * keep
    a_scr[0, MIX_HALO:ext, :] = av_ref[...] * jax.nn.sigmoid(ag_ref[...])
    for r in range(1, SUBLANES):
        a_scr[r, 0:ext - SUBLANES, :] = a_scr[0, r:r + ext - SUBLANES, :]

    base = MIX_HALO - (width - 1)

    def conv_body(rc, carry):
        r0 = pl.multiple_of(rc * conv_rows, conv_rows)
        for l0 in range(0, c, LANES):
            acc = jnp.broadcast_to(cb_ref[:, l0:l0 + LANES], (conv_rows, LANES))
            for k in range(width):
                off = base + k
                acc = acc + cw_ref[k:k + 1, l0:l0 + LANES] * a_scr[
                    off % SUBLANES, pl.ds(r0 + (off // SUBLANES) * SUBLANES, conv_rows), l0:l0 + LANES]
            y_scr[pl.ds(r0, conv_rows), l0:l0 + LANES] = acc
        return carry
    lax.fori_loop(0, ts // conv_rows, conv_body, 0)

    def ln_body(rc, carry):
        r0 = pl.multiple_of(rc * row_chunk, row_chunk)
        ya = _layernorm_rows(y_scr[pl.ds(r0, row_chunk), :], lag_ref[...], lab_ref[...])
        o_ref[pl.ds(r0, row_chunk), 0:c] = (ya * jax.nn.sigmoid(ya)).astype(o_ref.dtype)
        vv = _layernorm_rows(jax.nn.gelu(v_ref[pl.ds(r0, row_chunk), :]), lvg_ref[...], lvb_ref[...])
        v_scr[pl.ds(r0, row_chunk), :] = vv.astype(v_scr.dtype)
        return carry
    lax.fori_loop(0, ts // row_chunk, ln_body, 0)

    hd = c // heads
    row = lax.broadcasted_iota(jnp.int32, (chunk, chunk), 0)
    col = lax.broadcasted_iota(jnp.int32, (chunk, chunk), 1)
    for hh in range(heads):
        wm = jnp.where(col <= row, ws_ref[hh], 0.0).astype(BF16)
        bias = bst_ref[:, hh:hh + 1]
        for cc in range(ts // chunk):
            rows = slice(cc * chunk, (cc + 1) * chunk)
            cols = slice(hh * hd, (hh + 1) * hd)
            s = jnp.dot(wm, v_scr[rows, cols], preferred_element_type=F32) + bias
            o_ref[rows, c + hh * hd:c + (hh + 1) * hd] = (
                jax.nn.gelu(u_ref[rows, cols]) * s).astype(o_ref.dtype)


def even_mix(z, conv_w, conv_b, ln_a_g, ln_a_b, ln_v_g, ln_v_b, w_s, b_s_t, *, seq, ts):
    t = z.shape[0]
    c = z.shape[1] // 4
    width = conv_w.shape[0]
    heads, chunk, _ = w_s.shape
    assert width - 1 <= MIX_HALO and ts % chunk == 0 and seq % ts == 0
    halo_blocks = ts // MIX_HALO
    row = lambda i: (i, 0)
    vec = pl.BlockSpec((1, c), lambda i: (0, 0))
    return pl.pallas_call(
        functools.partial(_even_mix_kernel, tiles_per_seq=seq // ts, width=width, row_chunk=32,
                          conv_rows=32),
        grid=(t // ts,),
        in_specs=[
            pl.BlockSpec((ts, c), lambda i: (i, 0)),
            pl.BlockSpec((ts, c), lambda i: (i, 1)),
            pl.BlockSpec((MIX_HALO, c), lambda i: (jnp.maximum(i * halo_blocks - 1, 0), 0)),
            pl.BlockSpec((MIX_HALO, c), lambda i: (jnp.maximum(i * halo_blocks - 1, 0), 1)),
            pl.BlockSpec((ts, c), lambda i: (i, 2)),
            pl.BlockSpec((ts, c), lambda i: (i, 3)),
            pl.BlockSpec((width, c), lambda i: (0, 0)),
            vec, vec, vec, vec, vec,
            pl.BlockSpec((heads, chunk, chunk), lambda i: (0, 0, 0)),
            pl.BlockSpec((chunk, heads), lambda i: (0, 0)),
        ],
        out_specs=pl.BlockSpec((ts, 2 * c), row),
        out_shape=jax.ShapeDtypeStruct((t, 2 * c), BF16),
        scratch_shapes=[
            pltpu.VMEM((SUBLANES, ts + MIX_HALO, c), F32),
            pltpu.VMEM((ts, c), F32),
            pltpu.VMEM((ts, c), BF16),
        ],
        compiler_params=_params("parallel"),
        name="even_mix",
    )(z, z, z, z, z, z, conv_w, conv_b, ln_a_g, ln_a_b, ln_v_g, ln_v_b, w_s, b_s_t)


def _diff_attn_kernel(q_ref, k_ref, v_ref, lq1_ref, lk1_ref, lq2_ref, lk2_ref, sg_ref, o_ref,
                      m_scr, l_scr, acc_scr, s_scr, p_scr, alpha_scr, *, lambda_init, tk, sm_rows):
    i = pl.program_id(2)
    tq, dv = q_ref.shape
    dh = dv // 2
    lane_tiles = tk // LANES

    m_scr[...] = jnp.full(m_scr.shape, NEG_INF, F32)
    l_scr[...] = jnp.zeros(l_scr.shape, F32)
    acc_scr[...] = jnp.zeros(acc_scr.shape, F32)

    def step(j, r0, diagonal):
        k0 = pl.multiple_of(j * tk, tk)
        half = (tq - r0) // 2
        units = [(c, b0) for c in range(2) for b0 in (r0, r0 + half)]

        def qk(c, b0):
            rows = slice(b0, b0 + half)
            s_scr[c, rows, :] = lax.dot_general(
                q_ref[rows, c * dh:(c + 1) * dh], k_ref[pl.ds(k0, tk), c * dh:(c + 1) * dh],
                (((1,), (1,)), ((), ())), preferred_element_type=F32)

        def softmax_group(c, g0):
            rows = slice(g0, g0 + sm_rows)
            s = s_scr[c, rows, :]
            if diagonal and g0 - r0 < tk:
                rel = lax.broadcasted_iota(jnp.int32, (sm_rows, tk), 0) + (g0 - r0)
                col = lax.broadcasted_iota(jnp.int32, (sm_rows, tk), 1)
                s = jnp.where(col <= rel, s, NEG_INF)
            tiles = [s[:, t * LANES:(t + 1) * LANES] for t in range(lane_tiles)]
            tile_max = functools.reduce(jnp.maximum, tiles)
            m_prev = m_scr[c, rows, :]
            m_new = jnp.maximum(m_prev, jnp.max(tile_max, axis=-1, keepdims=True))
            alpha = jnp.exp2(m_prev - m_new)
            p_tiles = [jnp.exp2(t - m_new) for t in tiles]
            l_scr[c, rows, :] = alpha * l_scr[c, rows, :] + functools.reduce(jnp.add, p_tiles)
            p_scr[c, rows, :] = jnp.concatenate(p_tiles, axis=1).astype(BF16)
            alpha_scr[c, rows, :] = alpha
            m_scr[c, rows, :] = m_new

        def pv(c, b0):
            rows = slice(b0, b0 + half)
            alpha_v = jnp.concatenate([alpha_scr[c, rows, :]] * (dv // LANES), axis=1)
            acc_scr[c, rows, :] = alpha_v * acc_scr[c, rows, :] + jnp.dot(
                p_scr[c, rows, :], v_ref[pl.ds(k0, tk), :], preferred_element_type=F32)

        def sm(u):
            c, b0 = units[u]
            return [functools.partial(softmax_group, c, g0) for g0 in range(b0, b0 + half, sm_rows)]

        stages = [
            ([(qk, 0)], []),
            ([(qk, 1)], sm(0)),
            ([(qk, 2)], sm(1)),
            ([(pv, 0), (qk, 3)], sm(2)),
            ([(pv, 1), (pv, 2)], sm(3)),
            ([(pv, 3)], []),
        ]
        for pieces, groups in stages:
            per_piece = -(-len(groups) // len(pieces))
            for n, (fn, u) in enumerate(pieces):
                fn(*units[u])
                for grp in groups[n * per_piece:(n + 1) * per_piece]:
                    grp()

    n_full = i * (tq // tk)

    def body(j, carry):
        step(j, 0, False)
        return carry
    lax.fori_loop(0, n_full, body, 0)
    for d in range(tq // tk):
        step(n_full + d, d * tk, True)

    lam = (jnp.exp(jnp.sum(lq1_ref[...] * lk1_ref[...], axis=-1, keepdims=True))
           - jnp.exp(jnp.sum(lq2_ref[...] * lk2_ref[...], axis=-1, keepdims=True))
           + lambda_init)
    l1 = jnp.sum(l_scr[0], axis=-1, keepdims=True)
    l2 = jnp.sum(l_scr[1], axis=-1, keepdims=True)
    o = acc_scr[0] / l1 - lam * (acc_scr[1] / l2)
    o_ref[...] = (_rms_rows(o, sg_ref[...]) * (1.0 - lambda_init)).astype(o_ref.dtype)


def diff_attention(qkv, lq1, lk1, lq2, lk2, subln_g, *, batch, seq, heads, lambda_init, tq, tk):
    t = qkv.shape[0]
    dv = subln_g.shape[1]
    dh = lq1.shape[1]
    assert dv == 2 * dh and tq % tk == 0 and tk % LANES == 0 and tk % 64 == 0
    nq = seq // tq
    vec = lambda n: pl.BlockSpec((1, n), lambda b, h, i: (0, 0))
    return pl.pallas_call(
        functools.partial(_diff_attn_kernel, lambda_init=lambda_init, tk=tk, sm_rows=32),
        grid=(batch, heads, nq),
        in_specs=[
            pl.BlockSpec((tq, dv), lambda b, h, i: (b * nq + i, h)),
            pl.BlockSpec((seq, dv), lambda b, h, i: (b, heads + h)),
            pl.BlockSpec((seq, dv), lambda b, h, i: (b, 2 * heads + h)),
            vec(dh), vec(dh), vec(dh), vec(dh), vec(dv),
        ],
        out_specs=pl.BlockSpec((tq, dv), lambda b, h, i: (b * nq + i, h)),
        out_shape=jax.ShapeDtypeStruct((t, heads * dv), BF16),
        scratch_shapes=[
            pltpu.VMEM((2, tq, LANES), F32),
            pltpu.VMEM((2, tq, LANES), F32),
            pltpu.VMEM((2, tq, dv), F32),
            pltpu.VMEM((2, tq, tk), F32),
            pltpu.VMEM((2, tq, tk), BF16),
            pltpu.VMEM((2, tq, LANES), F32),
        ],
        compiler_params=_params("parallel", "parallel", "arbitrary"),
        name="diff_attention",
    )(qkv, qkv, qkv, lq1, lk1, lq2, lk2, subln_g)


def _tiles(seq, d_ff):
    return dict(
        proj_tm=min(512, seq), proj_tn=1024,
        ffn_tm=min(512, seq), ffn_tf=512 if d_ff % 512 == 0 else 128,
        mix_ts=min(256, seq), attn_tq=min(2048, seq), attn_tk=min(512, seq),
    )


def _forward(x, norm_mix_g, norm_ffn_g, final_norm_g, ev_w_in, ev_conv_w, ev_conv_b, ev_ln_a_g,
             ev_ln_a_b, ev_ln_v_g, ev_ln_v_b, ev_w_s, ev_b_s, ev_w_out, od_w_qkv, od_lambda_q1,
             od_lambda_k1, od_lambda_q2, od_lambda_k2, od_subln_g, od_w_o, ffn_w_up, ffn_conv_w,
             ffn_conv_b, ffn_w_down, tiles=None):
    batch, seq, d = x.shape
    depth = norm_mix_g.shape[0]
    dh = od_lambda_q1.shape[-1]
    dv = od_subln_g.shape[-1]
    heads = od_w_o.shape[1] // dv
    qk_w = heads * 2 * dh
    tl = dict(_tiles(seq, ffn_w_down.shape[1]))
    if tiles:
        tl.update(tiles)

    row = lambda v: v.reshape(1, -1)
    xt = x.reshape(batch * seq, d)
    w_in, w_out, w_qkv, w_o, w_up, w_down = (
        w.astype(BF16) for w in (ev_w_in, ev_w_out, od_w_qkv, od_w_o, ffn_w_up, ffn_w_down))
    for i in range(depth):
        j = i // 2
        if i % 2 == 0:
            ones = jnp.ones((1, w_in.shape[2]), F32)
            z = norm_matmul(xt, row(norm_mix_g[i]), w_in, j, ones, F32,
                            tm=tl["proj_tm"], tn=tl["proj_tn"])
            cat = even_mix(z, ev_conv_w[j], row(ev_conv_b[j]), row(ev_ln_a_g[j]), row(ev_ln_a_b[j]),
                           row(ev_ln_v_g[j]), row(ev_ln_v_b[j]), ev_w_s[j], ev_b_s[j].T,
                           seq=seq, ts=tl["mix_ts"])
            xt = matmul_residual(cat, w_out, j, xt, tm=tl["proj_tm"], tn=tl["proj_tn"])
        else:
            lambda_init = 0.8 - 0.6 * math.exp(-0.3 * i)
            col_scale = jnp.concatenate(
                [jnp.full((1, qk_w), math.log2(math.e) / math.sqrt(dh), F32),
                 jnp.ones((1, w_qkv.shape[2] - qk_w), F32)], axis=1)
            qkv = norm_matmul(xt, row(norm_mix_g[i]), w_qkv, j, col_scale, BF16,
                              tm=tl["proj_tm"], tn=tl["proj_tn"])
            o = diff_attention(qkv, row(od_lambda_q1[j]), row(od_lambda_k1[j]), row(od_lambda_q2[j]),
                               row(od_lambda_k2[j]), row(od_subln_g[j]), batch=batch, seq=seq,
                               heads=heads, lambda_init=lambda_init, tq=tl["attn_tq"], tk=tl["attn_tk"])
            xt = matmul_residual(o, w_o, j, xt, tm=tl["proj_tm"], tn=tl["proj_tn"])
        xt = conv_ffn(xt, row(norm_ffn_g[i]), w_up, ffn_conv_w[i], row(ffn_conv_b[i]), w_down, i,
                      row(final_norm_g), seq=seq, tm=tl["ffn_tm"], tf=tl["ffn_tf"],
                      out_norm=(i == depth - 1))
    return xt.reshape(batch, seq, d)


def kernel(x, norm_mix_g, norm_ffn_g, final_norm_g, ev_w_in, ev_conv_w, ev_conv_b, ev_ln_a_g, ev_ln_a_b, ev_ln_v_g, ev_ln_v_b, ev_w_s, ev_b_s, ev_w_out, od_w_qkv, od_lambda_q1, od_lambda_k1, od_lambda_q2, od_lambda_k2, od_subln_g, od_w_o, ffn_w_up, ffn_conv_w, ffn_conv_b, ffn_w_down):
    return _forward(x, norm_mix_g, norm_ffn_g, final_norm_g, ev_w_in, ev_conv_w, ev_conv_b, ev_ln_a_g,
                    ev_ln_a_b, ev_ln_v_g, ev_ln_v_b, ev_w_s, ev_b_s, ev_w_out, od_w_qkv, od_lambda_q1,
                    od_lambda_k1, od_lambda_q2, od_lambda_k2, od_subln_g, od_w_o, ffn_w_up,
                    ffn_conv_w, ffn_conv_b, ffn_w_down)
```

```python
import functools
import math

import jax
import jax.numpy as jnp
from jax import lax
from jax.experimental import pallas as pl
from jax.experimental.pallas import tpu as pltpu

EPS = 1e-6
NEG_INF = -1e30
V7X_VMEM_BYTES = 64 * 1024 * 1024
VMEM_LIMIT_BYTES = V7X_VMEM_BYTES - 8 * 1024 * 1024
SUBLANES = 8
LANES = 128
BF16_SUBLANES = 16

F32 = jnp.float32
BF16 = jnp.bfloat16


def _params(*sem):
    return pltpu.CompilerParams(dimension_semantics=sem, vmem_limit_bytes=VMEM_LIMIT_BYTES)


def _rms_rows(x, g):
    ms = jnp.mean(x * x, axis=-1, keepdims=True)
    return x * lax.rsqrt(ms + EPS) * g


def _layernorm_rows(x, g, b):
    mu = jnp.mean(x, axis=-1, keepdims=True)
    xc = x - mu
    var = jnp.mean(xc * xc, axis=-1, keepdims=True)
    return xc * lax.rsqrt(var + EPS) * g + b


def _norm_matmul_kernel(x_ref, g_ref, w_ref, cs_ref, o_ref, h_ref, *, row_chunk, tn):
    def body(r, c):
        r0 = pl.multiple_of(r * row_chunk, row_chunk)
        h_ref[pl.ds(r0, row_chunk), :] = _rms_rows(
            x_ref[pl.ds(r0, row_chunk), :], g_ref[...]).astype(h_ref.dtype)
        return c
    lax.fori_loop(0, x_ref.shape[0] // row_chunk, body, 0)

    for n0 in range(0, o_ref.shape[1], tn):
        cols = slice(n0, n0 + tn)
        acc = jnp.dot(h_ref[...], w_ref[:, cols], preferred_element_type=F32)
        o_ref[:, cols] = (acc * cs_ref[:, cols]).astype(o_ref.dtype)


def norm_matmul(x, g, w, layer, col_scale, out_dtype, *, tm, tn):
    t, d = x.shape
    n = w.shape[2]
    return pl.pallas_call(
        functools.partial(_norm_matmul_kernel, row_chunk=min(tm, 128), tn=tn),
        grid=(t // tm,),
        in_specs=[
            pl.BlockSpec((tm, d), lambda i: (i, 0)),
            pl.BlockSpec((1, d), lambda i: (0, 0)),
            pl.BlockSpec((None, d, n), lambda i: (layer, 0, 0), pipeline_mode=pl.Buffered(1)),
            pl.BlockSpec((1, n), lambda i: (0, 0)),
        ],
        out_specs=pl.BlockSpec((tm, n), lambda i: (i, 0)),
        out_shape=jax.ShapeDtypeStruct((t, n), out_dtype),
        scratch_shapes=[pltpu.VMEM((tm, d), BF16)],
        compiler_params=_params("parallel"),
        name="norm_matmul",
    )(x, g, w, col_scale)


def _matmul_res_kernel(a_ref, w_ref, x_ref, o_ref, *, tn):
    for n0 in range(0, o_ref.shape[1], tn):
        cols = slice(n0, n0 + tn)
        o_ref[:, cols] = x_ref[:, cols] + jnp.dot(a_ref[...], w_ref[:, cols],
                                                  preferred_element_type=F32)


def matmul_residual(a, w, layer, x, *, tm, tn):
    t, k = a.shape
    n = w.shape[2]
    return pl.pallas_call(
        functools.partial(_matmul_res_kernel, tn=tn),
        grid=(t // tm,),
        in_specs=[
            pl.BlockSpec((tm, k), lambda i: (i, 0)),
            pl.BlockSpec((None, k, n), lambda i: (layer, 0, 0), pipeline_mode=pl.Buffered(1)),
            pl.BlockSpec((tm, n), lambda i: (i, 0)),
        ],
        out_specs=pl.BlockSpec((tm, n), lambda i: (i, 0)),
        out_shape=jax.ShapeDtypeStruct((t, n), F32),
        compiler_params=_params("parallel"),
        name="matmul_residual",
    )(a, w, x)


FFN_HALO = BF16_SUBLANES


def _ffn_kernel(x_ref, xh_ref, g_ref, wg_ref, wv_ref, cwg_ref, cwv_ref, cbg_ref, cbv_ref,
                wd_ref, og_ref, o_ref, h_ref, *, tiles_per_seq, row_chunk, out_norm):
    i = pl.program_id(0)
    f = pl.program_id(1)
    tm = x_ref.shape[0]

    @pl.when(f == 0)
    def _():
        keep = ((i % tiles_per_seq) != 0).astype(F32)
        h_ref[0:FFN_HALO, :] = (_rms_rows(xh_ref[...], g_ref[...]) * keep).astype(h_ref.dtype)

        def body(r, c):
            r0 = pl.multiple_of(r * row_chunk, row_chunk)
            xr = x_ref[pl.ds(r0, row_chunk), :]
            h_ref[pl.ds(FFN_HALO + r0, row_chunk), :] = _rms_rows(xr, g_ref[...]).astype(h_ref.dtype)
            o_ref[pl.ds(r0, row_chunk), :] = xr
            return c
        lax.fori_loop(0, tm // row_chunk, body, 0)

    h = h_ref[...]

    def conv(z, cw_ref, cb_ref):
        z1 = pltpu.roll(z, 1, 0)
        z2 = pltpu.roll(z, 2, 0)
        y = cw_ref[2:3, :] * z + cw_ref[1:2, :] * z1 + cw_ref[0:1, :] * z2 + cb_ref[...]
        return y[FFN_HALO:, :]

    gate = conv(jnp.dot(h, wg_ref[...], preferred_element_type=F32), cwg_ref, cbg_ref)
    val = conv(jnp.dot(h, wv_ref[...], preferred_element_type=F32), cwv_ref, cbv_ref)
    act = (gate * jax.nn.sigmoid(gate) * val).astype(BF16)
    o_ref[...] += jnp.dot(act, wd_ref[...], preferred_element_type=F32)

    if out_norm:
        @pl.when(f == pl.num_programs(1) - 1)
        def _():
            def body(r, c):
                r0 = pl.multiple_of(r * row_chunk, row_chunk)
                o_ref[pl.ds(r0, row_chunk), :] = _rms_rows(o_ref[pl.ds(r0, row_chunk), :], og_ref[...])
                return c
            lax.fori_loop(0, tm // row_chunk, body, 0)


def conv_ffn(x, g, w_up, conv_w, conv_b, w_down, layer, out_g, *, seq, tm, tf, out_norm):
    t, d = x.shape
    dff = w_down.shape[1]
    nf = dff // tf
    halo_blocks = tm // FFN_HALO
    return pl.pallas_call(
        functools.partial(_ffn_kernel, tiles_per_seq=seq // tm, row_chunk=min(tm, 128),
                          out_norm=out_norm),
        grid=(t // tm, nf),
        in_specs=[
            pl.BlockSpec((tm, d), lambda i, f: (i, 0)),
            pl.BlockSpec((FFN_HALO, d), lambda i, f: (jnp.maximum(i * halo_blocks - 1, 0), 0)),
            pl.BlockSpec((1, d), lambda i, f: (0, 0)),
            pl.BlockSpec((None, d, tf), lambda i, f: (layer, 0, f)),
            pl.BlockSpec((None, d, tf), lambda i, f: (layer, 0, f + nf)),
            pl.BlockSpec((3, tf), lambda i, f: (0, f)),
            pl.BlockSpec((3, tf), lambda i, f: (0, f + nf)),
            pl.BlockSpec((1, tf), lambda i, f: (0, f)),
            pl.BlockSpec((1, tf), lambda i, f: (0, f + nf)),
            pl.BlockSpec((None, tf, d), lambda i, f: (layer, f, 0)),
            pl.BlockSpec((1, d), lambda i, f: (0, 0)),
        ],
        out_specs=pl.BlockSpec((tm, d), lambda i, f: (i, 0)),
        out_shape=jax.ShapeDtypeStruct((t, d), F32),
        scratch_shapes=[pltpu.VMEM((FFN_HALO + tm, d), BF16)],
        compiler_params=_params("parallel", "arbitrary"),
        name="conv_ffn",
    )(x, x, g, w_up, w_up, conv_w, conv_w, conv_b, conv_b, w_down, out_g)


MIX_HALO = 32


def _even_mix_kernel(av_ref, ag_ref, avh_ref, agh_ref, u_ref, v_ref, cw_ref, cb_ref,
                     lag_ref, lab_ref, lvg_ref, lvb_ref, ws_ref, bst_ref, o_ref,
                     a_scr, y_scr, v_scr, *, tiles_per_seq, width, row_chunk, conv_rows):
    i = pl.program_id(0)
    ts, c = av_ref.shape
    ext = ts + MIX_HALO
    heads, chunk, _ = ws_ref.shape

    keep = ((i % tiles_per_seq) != 0).astype(F32)
    a_scr[0, 0:MIX_HALO, :] = avh_ref[...] * jax.nn.sigmoid(agh_ref[...]) * keep
    a_scr[0, MIX_HALO:ext, :] = av_ref[...] * jax.nn.sigmoid(ag_ref[...])
    for r in range(1, SUBLANES):
        a_scr[r, 0:ext - SUBLANES, :] = a_scr[0, r:r + ext - SUBLANES, :]

    base = MIX_HALO - (width - 1)

    def conv_body(rc, carry):
        r0 = pl.multiple_of(rc * conv_rows, conv_rows)
        for l0 in range(0, c, LANES):
            acc = jnp.broadcast_to(cb_ref[:, l0:l0 + LANES], (conv_rows, LANES))
            for k in range(width):
                off = base + k
                acc = acc + cw_ref[k:k + 1, l0:l0 + LANES] * a_scr[
                    off % SUBLANES, pl.ds(r0 + (off // SUBLANES) * SUBLANES, conv_rows), l0:l0 + LANES]
            y_scr[pl.ds(r0, conv_rows), l0:l0 + LANES] = acc
        return carry
    lax.fori_loop(0, ts // conv_rows, conv_body, 0)

    def ln_body(rc, carry):
        r0 = pl.multiple_of(rc * row_chunk, row_chunk)
        ya = _layernorm_rows(y_scr[pl.ds(r0, row_chunk), :], lag_ref[...], lab_ref[...])
        o_ref[pl.ds(r0, row_chunk), 0:c] = (ya * jax.nn.sigmoid(ya)).astype(o_ref.dtype)
        vv = _layernorm_rows(jax.nn.gelu(v_ref[pl.ds(r0, row_chunk), :]), lvg_ref[...], lvb_ref[...])
        v_scr[pl.ds(r0, row_chunk), :] = vv.astype(v_scr.dtype)
        return carry
    lax.fori_loop(0, ts // row_chunk, ln_body, 0)

    hd = c // heads
    row = lax.broadcasted_iota(jnp.int32, (chunk, chunk), 0)
    col = lax.broadcasted_iota(jnp.int32, (chunk, chunk), 1)
    for hh in range(heads):
        wm = jnp.where(col <= row, ws_ref[hh], 0.0).astype(BF16)
        bias = bst_ref[:, hh:hh + 1]
        for cc in range(ts // chunk):
            rows = slice(cc * chunk, (cc + 1) * chunk)
            cols = slice(hh * hd, (hh + 1) * hd)
            s = jnp.dot(wm, v_scr[rows, cols], preferred_element_type=F32) + bias
            o_ref[rows, c + hh * hd:c + (hh + 1) * hd] = (
                jax.nn.gelu(u_ref[rows, cols]) * s).astype(o_ref.dtype)


def even_mix(z, conv_w, conv_b, ln_a_g, ln_a_b, ln_v_g, ln_v_b, w_s, b_s_t, *, seq, ts):
    t = z.shape[0]
    c = z.shape[1] // 4
    width = conv_w.shape[0]
    heads, chunk, _ = w_s.shape
    assert width - 1 <= MIX_HALO and ts % chunk == 0 and seq % ts == 0
    halo_blocks = ts // MIX_HALO
    row = lambda i: (i, 0)
    vec = pl.BlockSpec((1, c), lambda i: (0, 0))
    return pl.pallas_call(
        functools.partial(_even_mix_kernel, tiles_per_seq=seq // ts, width=width, row_chunk=32,
                          conv_rows=32),
        grid=(t // ts,),
        in_specs=[
            pl.BlockSpec((ts, c), lambda i: (i, 0)),
            pl.BlockSpec((ts, c), lambda i: (i, 1)),
            pl.BlockSpec((MIX_HALO, c), lambda i: (jnp.maximum(i * halo_blocks - 1, 0), 0)),
            pl.BlockSpec((MIX_HALO, c), lambda i: (jnp.maximum(i * halo_blocks - 1, 0), 1)),
            pl.BlockSpec((ts, c), lambda i: (i, 2)),
            pl.BlockSpec((ts, c), lambda i: (i, 3)),
            pl.BlockSpec((width, c), lambda i: (0, 0)),
            vec, vec, vec, vec, vec,
            pl.BlockSpec((heads, chunk, chunk), lambda i: (0, 0, 0)),
            pl.BlockSpec((chunk, heads), lambda i: (0, 0)),
        ],
        out_specs=pl.BlockSpec((ts, 2 * c), row),
        out_shape=jax.ShapeDtypeStruct((t, 2 * c), BF16),
        scratch_shapes=[
            pltpu.VMEM((SUBLANES, ts + MIX_HALO, c), F32),
            pltpu.VMEM((ts, c), F32),
            pltpu.VMEM((ts, c), BF16),
        ],
        compiler_params=_params("parallel"),
        name="even_mix",
    )(z, z, z, z, z, z, conv_w, conv_b, ln_a_g, ln_a_b, ln_v_g, ln_v_b, w_s, b_s_t)


def _diff_attn_kernel(q_ref, k_ref, v_ref, lq1_ref, lk1_ref, lq2_ref, lk2_ref, sg_ref, o_ref,
                      m_scr, l_scr, acc_scr, s_scr, p_scr, alpha_scr, *, lambda_init, tk, sm_rows):
    i = pl.program_id(2)
    tq, dv = q_ref.shape
    dh = dv // 2
    lane_tiles = tk // LANES

    m_scr[...] = jnp.full(m_scr.shape, NEG_INF, F32)
    l_scr[...] = jnp.zeros(l_scr.shape, F32)
    acc_scr[...] = jnp.zeros(acc_scr.shape, F32)

    def step(j, r0, diagonal):
        k0 = pl.multiple_of(j * tk, tk)
        half = (tq - r0) // 2
        units = [(c, b0) for c in range(2) for b0 in (r0, r0 + half)]

        def qk(c, b0):
            rows = slice(b0, b0 + half)
            s_scr[c, rows, :] = lax.dot_general(
                q_ref[rows, c * dh:(c + 1) * dh], k_ref[pl.ds(k0, tk), c * dh:(c + 1) * dh],
                (((1,), (1,)), ((), ())), preferred_element_type=F32)

        def softmax_group(c, g0):
            rows = slice(g0, g0 + sm_rows)
            s = s_scr[c, rows, :]
            if diagonal and g0 - r0 < tk:
                rel = lax.broadcasted_iota(jnp.int32, (sm_rows, tk), 0) + (g0 - r0)
                col = lax.broadcasted_iota(jnp.int32, (sm_rows, tk), 1)
                s = jnp.where(col <= rel, s, NEG_INF)
            tiles = [s[:, t * LANES:(t + 1) * LANES] for t in range(lane_tiles)]
            tile_max = functools.reduce(jnp.maximum, tiles)
            m_prev = m_scr[c, rows, :]
            m_new = jnp.maximum(m_prev, jnp.max(tile_max, axis=-1, keepdims=True))
            alpha = jnp.exp2(m_prev - m_new)
            p_tiles = [jnp.exp2(t - m_new) for t in tiles]
            l_scr[c, rows, :] = alpha * l_scr[c, rows, :] + functools.reduce(jnp.add, p_tiles)
            p_scr[c, rows, :] = jnp.concatenate(p_tiles, axis=1).astype(BF16)
            alpha_scr[c, rows, :] = alpha
            m_scr[c, rows, :] = m_new

        def pv(c, b0):
            rows = slice(b0, b0 + half)
            alpha_v = jnp.concatenate([alpha_scr[c, rows, :]] * (dv // LANES), axis=1)
            acc_scr[c, rows, :] = alpha_v * acc_scr[c, rows, :] + jnp.dot(
                p_scr[c, rows, :], v_ref[pl.ds(k0, tk), :], preferred_element_type=F32)

        def sm(u):
            c, b0 = units[u]
            return [functools.partial(softmax_group, c, g0) for g0 in range(b0, b0 + half, sm_rows)]

        stages = [
            ([(qk, 0)], []),
            ([(qk, 1)], sm(0)),
            ([(qk, 2)], sm(1)),
            ([(pv, 0), (qk, 3)], sm(2)),
            ([(pv, 1), (pv, 2)], sm(3)),
            ([(pv, 3)], []),
        ]
        for pieces, groups in stages:
            per_piece = -(-len(groups) // len(pieces))
            for n, (fn, u) in enumerate(pieces):
                fn(*units[u])
                for grp in groups[n * per_piece:(n + 1) * per_piece]:
                    grp()

    n_full = i * (tq // tk)

    def body(j, carry):
        step(j, 0, False)
        return carry
    lax.fori_loop(0, n_full, body, 0)
    for d in range(tq // tk):
        step(n_full + d, d * tk, True)

    lam = (jnp.exp(jnp.sum(lq1_ref[...] * lk1_ref[...], axis=-1, keepdims=True))
           - jnp.exp(jnp.sum(lq2_ref[...] * lk2_ref[...], axis=-1, keepdims=True))
           + lambda_init)
    l1 = jnp.sum(l_scr[0], axis=-1, keepdims=True)
    l2 = jnp.sum(l_scr[1], axis=-1, keepdims=True)
    o = acc_scr[0] / l1 - lam * (acc_scr[1] / l2)
    o_ref[...] = (_rms_rows(o, sg_ref[...]) * (1.0 - lambda_init)).astype(o_ref.dtype)


def diff_attention(qkv, lq1, lk1, lq2, lk2, subln_g, *, batch, seq, heads, lambda_init, tq, tk):
    t = qkv.shape[0]
    dv = subln_g.shape[1]
    dh = lq1.shape[1]
    assert dv == 2 * dh and tq % tk == 0 and tk % LANES == 0 and tk % 64 == 0
    nq = seq // tq
    vec = lambda n: pl.BlockSpec((1, n), lambda b, h, i: (0, 0))
    return pl.pallas_call(
        functools.partial(_diff_attn_kernel, lambda_init=lambda_init, tk=tk, sm_rows=32),
        grid=(batch, heads, nq),
        in_specs=[
            pl.BlockSpec((tq, dv), lambda b, h, i: (b * nq + i, h)),
            pl.BlockSpec((seq, dv), lambda b, h, i: (b, heads + h)),
            pl.BlockSpec((seq, dv), lambda b, h, i: (b, 2 * heads + h)),
            vec(dh), vec(dh), vec(dh), vec(dh), vec(dv),
        ],
        out_specs=pl.BlockSpec((tq, dv), lambda b, h, i: (b * nq + i, h)),
        out_shape=jax.ShapeDtypeStruct((t, heads * dv), BF16),
        scratch_shapes=[
            pltpu.VMEM((2, tq, LANES), F32),
            pltpu.VMEM((2, tq, LANES), F32),
            pltpu.VMEM((2, tq, dv), F32),
            pltpu.VMEM((2, tq, tk), F32),
            pltpu.VMEM((2, tq, tk), BF16),
            pltpu.VMEM((2, tq, LANES), F32),
        ],
        compiler_params=_params("parallel", "parallel", "arbitrary"),
        name="diff_attention",
    )(qkv, qkv, qkv, lq1, lk1, lq2, lk2, subln_g)


def _tiles(seq, d_ff):
    return dict(
        proj_tm=min(512, seq), proj_tn=1024,
        ffn_tm=min(1024, seq), ffn_tf=512 if d_ff % 512 == 0 else 128,
        mix_ts=min(256, seq), attn_tq=min(2048, seq), attn_tk=min(512, seq),
    )


def _forward(x, norm_mix_g, norm_ffn_g, final_norm_g, ev_w_in, ev_conv_w, ev_conv_b, ev_ln_a_g,
             ev_ln_a_b, ev_ln_v_g, ev_ln_v_b, ev_w_s, ev_b_s, ev_w_out, od_w_qkv, od_lambda_q1,
             od_lambda_k1, od_lambda_q2, od_lambda_k2, od_subln_g, od_w_o, ffn_w_up, ffn_conv_w,
             ffn_conv_b, ffn_w_down, tiles=None):
    batch, seq, d = x.shape
    depth = norm_mix_g.shape[0]
    dh = od_lambda_q1.shape[-1]
    dv = od_subln_g.shape[-1]
    heads = od_w_o.shape[1] // dv
    qk_w = heads * 2 * dh
    tl = dict(_tiles(seq, ffn_w_down.shape[1]))
    if tiles:
        tl.update(tiles)

    row = lambda v: v.reshape(1, -1)
    xt = x.reshape(batch * seq, d)
    w_in, w_out, w_qkv, w_o, w_up, w_down = (
        w.astype(BF16) for w in (ev_w_in, ev_w_out, od_w_qkv, od_w_o, ffn_w_up, ffn_w_down))
    for i in range(depth):
        j = i // 2
        if i % 2 == 0:
            ones = jnp.ones((1, w_in.shape[2]), F32)
            z = norm_matmul(xt, row(norm_mix_g[i]), w_in, j, ones, F32,
                            tm=tl["proj_tm"], tn=tl["proj_tn"])
            cat = even_mix(z, ev_conv_w[j], row(ev_conv_b[j]), row(ev_ln_a_g[j]), row(ev_ln_a_b[j]),
                           row(ev_ln_v_g[j]), row(ev_ln_v_b[j]), ev_w_s[j], ev_b_s[j].T,
                           seq=seq, ts=tl["mix_ts"])
            xt = matmul_residual(cat, w_out, j, xt, tm=tl["proj_tm"], tn=tl["proj_tn"])
        else:
            lambda_init = 0.8 - 0.6 * math.exp(-0.3 * i)
            col_scale = jnp.concatenate(
                [jnp.full((1, qk_w), math.log2(math.e) / math.sqrt(dh), F32),
                 jnp.ones((1, w_qkv.shape[2] - qk_w), F32)], axis=1)
            qkv = norm_matmul(xt, row(norm_mix_g[i]), w_qkv, j, col_scale, BF16,
                              tm=tl["proj_tm"], tn=tl["proj_tn"])
            o = diff_attention(qkv, row(od_lambda_q1[j]), row(od_lambda_k1[j]), row(od_lambda_q2[j]),
                               row(od_lambda_k2[j]), row(od_subln_g[j]), batch=batch, seq=seq,
                               heads=heads, lambda_init=lambda_init, tq=tl["attn_tq"], tk=tl["attn_tk"])
            xt = matmul_residual(o, w_o, j, xt, tm=tl["proj_tm"], tn=tl["proj_tn"])
        xt = conv_ffn(xt, row(norm_ffn_g[i]), w_up, ffn_conv_w[i], row(ffn_conv_b[i]), w_down, i,
                      row(final_norm_g), seq=seq, tm=tl["ffn_tm"], tf=tl["ffn_tf"],
                      out_norm=(i == depth - 1))
    return xt.reshape(batch, seq, d)


def kernel(x, norm_mix_g, norm_ffn_g, final_norm_g, ev_w_in, ev_conv_w, ev_conv_b, ev_ln_a_g, ev_ln_a_b, ev_ln_v_g, ev_ln_v_b, ev_w_s, ev_b_s, ev_w_out, od_w_qkv, od_lambda_q1, od_lambda_k1, od_lambda_q2, od_lambda_k2, od_subln_g, od_w_o, ffn_w_up, ffn_conv_w, ffn_conv_b, ffn_w_down):
    return _forward(x, norm_mix_g, norm_ffn_g, final_norm_g, ev_w_in, ev_conv_w, ev_conv_b, ev_ln_a_g,
                    ev_ln_a_b, ev_ln_v_g, ev_ln_v_b, ev_w_s, ev_b_s, ev_w_out, od_w_qkv, od_lambda_q1,
                    od_lambda_k1, od_lambda_q2, od_lambda_k2, od_subln_g, od_w_o, ffn_w_up,
                    ffn_conv_w, ffn_conv_b, ffn_w_down)
```
